```python
import functools
import jax, jax.numpy as jnp
from jax import lax
import numpy as np


D_MODEL = 1024
BATCH = 2
SEQ = 8192
DEPTH = 2
DEC_BATCH = 128
DEC_SEQ = 4
PAST_LEN = 16384
PAGE_SIZE = 128

N_EVEN = (DEPTH + 1) // 2
N_ODD = DEPTH // 2
MLA_HEADS = 8
MLA_NOPE = 64
MLA_ROPE = 32
MLA_V = 64
MLA_KV_LORA = 256
MLA_Q_LORA = 384
ROPE_THETA = 10000.0
HG_HEADS = 4
HG_DK = 128
HG_DV = 128
HG_CHUNK = 64
FOX_HEADS = 16
FOX_KV_HEADS = 4
FOX_HD = 64
FOX_F_BIAS_INIT = 2.0
MEM_TOKENS = 256
MEM_HEADS = 4
MEM_HD = 128
MEM_INNER = MEM_HEADS * MEM_HD
D_FF = 2816
Q_BLOCK = 128
EPS = 1e-6

AB_SIZES = (MLA_Q_LORA, MLA_KV_LORA, MLA_ROPE, HG_HEADS * HG_DK, HG_HEADS * HG_DK, HG_HEADS * HG_DV, HG_HEADS * HG_DV)
D_IN_AB = MLA_Q_LORA + MLA_KV_LORA + MLA_ROPE + 2 * HG_HEADS * HG_DK + 2 * HG_HEADS * HG_DV
D_MIX_AB = MLA_HEADS * MLA_V + HG_HEADS * HG_DV
FOX_SIZES = (FOX_HEADS * FOX_HD, FOX_KV_HEADS * FOX_HD, FOX_KV_HEADS * FOX_HD, FOX_HEADS)
D_IN_C = FOX_HEADS * FOX_HD + 2 * FOX_KV_HEADS * FOX_HD + FOX_HEADS

N_FFN1_PRE = 0
N_FFN1_POST = 1
N_MIX_PRE = 2
N_MIX_POST = 3
N_MEM_PRE = 4
N_MEM_POST = 5
N_FFN2_PRE = 6
N_FFN2_POST = 7
N_MEM_SRC = 8
N_NORMS = 9

kernel_name = 'hybrid_mla_hgrn2_fox_macaron_step'


def rmsnorm(x, g):
    xf = x.astype(jnp.float32)
    y = xf * lax.rsqrt(jnp.mean(xf * xf, axis=-1, keepdims=True) + EPS)
    return (y * g.astype(jnp.float32)).astype(x.dtype)


def swiglu(x, w_gate, w_up, w_down):
    return (jax.nn.silu(x @ w_gate) * (x @ w_up)) @ w_down


def split_cols(x, sizes):
    out, start = [], 0
    for s in sizes:
        out.append(x[..., start:start + s])
        start += s
    return out


def apply_rope(x, pos):
    half = x.shape[-1] // 2
    inv_freq = ROPE_THETA ** (-jnp.arange(half, dtype=jnp.float32) / half)
    ang = pos.astype(jnp.float32)[:, None] * inv_freq[None, :]
    bshape = (1, x.shape[1]) + (1,) * (x.ndim - 3) + (half,)
    cos = jnp.cos(ang).reshape(bshape)
    sin = jnp.sin(ang).reshape(bshape)
    xf = x.astype(jnp.float32)
    x1, x2 = xf[..., :half], xf[..., half:]
    return jnp.concatenate([x1 * cos - x2 * sin, x1 * sin + x2 * cos], axis=-1).astype(x.dtype)


def online_update(carry, s, v, spec):
    m, l, acc = carry
    m_new = jnp.maximum(m, jnp.max(s, axis=-1))
    alpha = jnp.exp(m - m_new)
    p = jnp.exp(s - m_new[..., None])
    return (m_new, l * alpha + jnp.sum(p, axis=-1), acc * alpha[..., None] + jnp.einsum(spec, p, v.astype(jnp.float32)))


def mla_prompt_attend(q_nope, q_rope, c, kr, w_uk, w_uv):
    B, T, H, _ = q_nope.shape
    scale = (MLA_NOPE + MLA_ROPE) ** -0.5
    k_nope = jnp.einsum('bkc,chn->bkhn', c, w_uk)
    v = jnp.einsum('bkc,chv->bkhv', c, w_uv)
    kpos = jnp.arange(T)

    def block(i):
        start = i * Q_BLOCK
        qn = lax.dynamic_slice_in_dim(q_nope, start, Q_BLOCK, axis=1)
        qr = lax.dynamic_slice_in_dim(q_rope, start, Q_BLOCK, axis=1)
        s = (jnp.einsum('bqhn,bkhn->bhqk', qn, k_nope) + jnp.einsum('bqhr,bkr->bhqk', qr, kr)).astype(jnp.float32) * scale
        qpos = start + jnp.arange(Q_BLOCK)
        s = jnp.where(kpos[None, :] <= qpos[:, None], s, -jnp.inf)
        p = jax.nn.softmax(s, axis=-1).astype(v.dtype)
        return jnp.einsum('bhqk,bkhv->bqhv', p, v)

    o = lax.map(block, jnp.arange(T // Q_BLOCK))
    return jnp.moveaxis(o, 0, 1).reshape(B, T, H * MLA_V)


def mla_sample_attend(q_nope, q_rope, c, kr, w_uk, w_uv, lat_pool, kr_pool, layer, page_table):
    f32 = jnp.float32
    B, T, H, _ = q_nope.shape
    scale = (MLA_NOPE + MLA_ROPE) ** -0.5
    qa = jnp.einsum('bthn,chn->bthc', q_nope, w_uk).astype(f32)
    qr = q_rope.astype(f32)

    def scores(ck, kk):
        return (jnp.einsum('bthc,bkc->bhtk', qa, ck.astype(f32)) + jnp.einsum('bthr,bkr->bhtk', qr, kk.astype(f32))) * scale

    def step(carry, pages):
        ck = lat_pool[layer, pages]
        kk = kr_pool[layer, pages]
        return online_update(carry, scores(ck, kk), ck, 'bhtk,bkc->bhtc'), None

    init = (jnp.full((B, H, T), -jnp.inf, f32), jnp.zeros((B, H, T), f32), jnp.zeros((B, H, T, MLA_KV_LORA), f32))
    carry, _ = lax.scan(step, init, page_table.T)
    causal = jnp.tril(jnp.ones((T, T), dtype=bool))
    s_new = jnp.where(causal, scores(c, kr), -jnp.inf)
    _, l, acc = online_update(carry, s_new, c, 'bhtk,bkc->bhtc')
    o = jnp.einsum('bhtc,chv->bthv', acc / l[..., None], w_uv.astype(f32))
    return o.reshape(B, T, H * MLA_V).astype(q_nope.dtype)


def hgrn2_chunked(q, k, v, logf, s0):
    B, T, H, DK = q.shape
    DV = v.shape[-1]
    c = min(HG_CHUNK, T)
    n = T // c
    f32 = jnp.float32

    def chunks(a):
        return jnp.moveaxis(a.astype(f32).reshape((B, n, c) + a.shape[2:]), 1, 0)

    causal = jnp.tril(jnp.ones((c, c), dtype=bool))[None, :, :, None, None]

    def step(S, inp):
        qc, kc, vc, gc = inp
        cum = jnp.cumsum(gc, axis=1)
        diff = cum[:, :, None] - cum[:, None, :]
        decay = jnp.exp(jnp.where(causal, diff, -jnp.inf))
        a = jnp.einsum('bthk,bshk,btshk->bhts', qc, kc, decay)
        o = jnp.einsum('bhts,bshv->bthv', a, vc) + jnp.einsum('bthk,bhkv->bthv', qc * jnp.exp(cum), S)
        last = cum[:, -1]
        S = jnp.exp(last)[..., None] * S + jnp.einsum('bshk,bshv->bhkv', kc * jnp.exp(last[:, None] - cum), vc)
        return S, o

    S, o = lax.scan(step, s0.astype(f32), (chunks(q), chunks(k), chunks(v), chunks(logf)))
    return jnp.moveaxis(o, 0, 1).reshape(B, T, H, DV), S


def even_mixer(h, pos, attend, s0, w_in, g_q, g_kv, w_uq, w_uk, w_uv, lb, g_out, w_out):
    f32 = jnp.float32
    B, T, _ = h.shape
    q_lat, c_kv, kr_raw, hq, hf, hi, hg = split_cols(h @ w_in, AB_SIZES)
    q = (rmsnorm(q_lat, g_q) @ w_uq).reshape(B, T, MLA_HEADS, MLA_NOPE + MLA_ROPE)
    q_nope = q[..., :MLA_NOPE]
    q_rope = apply_rope(q[..., MLA_NOPE:], pos)
    c = rmsnorm(c_kv, g_kv)
    kr = apply_rope(kr_raw, pos)
    o_a = attend(q_nope, q_rope, c, kr, w_uk, w_uv)
    z = hf.astype(f32).reshape(B, T, HG_HEADS, HG_DK)
    lbh = lb.reshape(HG_HEADS, HG_DK)
    log_f = jnp.log(lbh + (1.0 - lbh) * jax.nn.sigmoid(z))
    k = (1.0 - lbh) * jax.nn.sigmoid(-z)
    o_b, S = hgrn2_chunked(hq.reshape(B, T, HG_HEADS, HG_DK), k, hi.reshape(B, T, HG_HEADS, HG_DV), log_f, s0)
    gate = jax.nn.silu(hg.astype(f32)).reshape(B, T, HG_HEADS, HG_DV)
    o_b = (rmsnorm(o_b, g_out.reshape(HG_HEADS, HG_DV)) * gate).reshape(B, T, HG_HEADS * HG_DV).astype(h.dtype)
    out = jnp.concatenate([o_a.astype(h.dtype), o_b], axis=-1) @ w_out
    return out, (c, kr, S.astype(h.dtype))


def fox_prompt_attend(q, k, v, logf):
    B, T, H, D = q.shape
    G = H // FOX_KV_HEADS
    scale = D ** -0.5
    qg = q.reshape(B, T, FOX_KV_HEADS, G, D)
    cum = jnp.cumsum(logf, axis=1).reshape(B, T, FOX_KV_HEADS, G).transpose(0, 2, 3, 1)
    kpos = jnp.arange(T)

    def block(i):
        start = i * Q_BLOCK
        qb = lax.dynamic_slice_in_dim(qg, start, Q_BLOCK, axis=1)
        cb = lax.dynamic_slice_in_dim(cum, start, Q_BLOCK, axis=3)
        s = jnp.einsum('bqngd,bknd->bngqk', qb, k).astype(jnp.float32) * scale + cb[..., None] - cum[..., None, :]
        qpos = start + jnp.arange(Q_BLOCK)
        s = jnp.where(kpos[None, :] <= qpos[:, None], s, -jnp.inf)
        p = jax.nn.softmax(s, axis=-1).astype(v.dtype)
        return jnp.einsum('bngqk,bknd->bqngd', p, v)

    o = lax.map(block, jnp.arange(T // Q_BLOCK))
    return jnp.moveaxis(o, 0, 1).reshape(B, T, H * D).astype(q.dtype)


def fox_sample_attend(q, k, v, logf, k_pool, v_pool, lf_pool, layer, page_table):
    f32 = jnp.float32
    B, T, H, D = q.shape
    N = FOX_KV_HEADS
    G = H // N
    n_pages = page_table.shape[1]
    scale = D ** -0.5
    qg = q.reshape(B, T, N, G, D).astype(f32)
    lf = lf_pool[layer, page_table].astype(f32).reshape(B, n_pages * PAGE_SIZE, H)
    suffix = lax.cumsum(lf, axis=1, reverse=True) - lf
    suffix = suffix.reshape(B, n_pages, PAGE_SIZE, N, G).transpose(1, 0, 3, 4, 2)
    cq = jnp.cumsum(logf, axis=1).reshape(B, T, N, G).transpose(0, 2, 3, 1)

    def step(carry, xs):
        pages, suf = xs
        kk = k_pool[layer, pages].astype(f32)
        vv = v_pool[layer, pages]
        s = jnp.einsum('btngd,bpnd->bngtp', qg, kk) * scale + cq[..., None] + suf[..., None, :]
        return online_update(carry, s, vv, 'bngtp,bpnd->bngtd'), None

    init = (jnp.full((B, N, G, T), -jnp.inf, f32), jnp.zeros((B, N, G, T), f32), jnp.zeros((B, N, G, T, D), f32))
    carry, _ = lax.scan(step, init, (page_table.T, suffix))
    causal = jnp.tril(jnp.ones((T, T), dtype=bool))
    s = jnp.einsum('btngd,bsnd->bngts', qg, k.astype(f32)) * scale + cq[..., :, None] - cq[..., None, :]
    s = jnp.where(causal, s, -jnp.inf)
    _, l, acc = online_update(carry, s, v, 'bngts,bsnd->bngtd')
    o = (acc / l[..., None]).transpose(0, 3, 1, 2, 4).reshape(B, T, H * D)
    return o.astype(q.dtype)


def odd_mixer(h, attend, w_in, b_f, w_out):
    B, T, _ = h.shape
    q, k, v, fz = split_cols(h @ w_in, FOX_SIZES)
    q = q.reshape(B, T, FOX_HEADS, FOX_HD)
    k = k.reshape(B, T, FOX_KV_HEADS, FOX_HD)
    v = v.reshape(B, T, FOX_KV_HEADS, FOX_HD)
    logf = jax.nn.log_sigmoid(fz.astype(jnp.float32) + b_f.astype(jnp.float32))
    o = attend(q, k, v, logf)
    return o @ w_out, (k, v, logf.astype(h.dtype))


def mem_kv(mem, g_src, w_k, w_v):
    B, M, _ = mem.shape
    m = rmsnorm(mem, g_src)
    return (m @ w_k).reshape(B, M, MEM_HEADS, MEM_HD), (m @ w_v).reshape(B, M, MEM_HEADS, MEM_HD)


def mem_attend(h, mk, mv, w_q, w_o):
    B, T, _ = h.shape
    q = (h @ w_q).reshape(B, T, MEM_HEADS, MEM_HD)
    s = jnp.einsum('bthd,bnhd->bhtn', q, mk).astype(jnp.float32) * MEM_HD ** -0.5
    p = jax.nn.softmax(s, axis=-1).astype(mv.dtype)
    return jnp.einsum('bhtn,bnhd->bthd', p, mv).reshape(B, T, MEM_INNER) @ w_o


def macaron_half(x, g_pre, g_post, w_gate, w_up, w_down):
    return x + 0.5 * rmsnorm(swiglu(rmsnorm(x, g_pre), w_gate, w_up, w_down), g_post)


def setup_inputs(seed: int = 0) -> dict:
    key = jax.random.key(seed)
    ks = iter(jax.random.split(key, 40))

    def nrm(shape, scale=1.0):
        return jax.random.normal(next(ks), shape, jnp.float32) * scale

    def gain(shape):
        return 1.0 + nrm(shape, 0.05)

    n_pages = PAST_LEN // PAGE_SIZE
    n_used = DEC_BATCH * n_pages
    n_phys = n_used + n_used // 4
    page_table = jax.random.permutation(next(ks), n_phys)[:n_used].reshape(DEC_BATCH, n_pages).astype(jnp.int32)
    return {
        'x_prompt': nrm((BATCH, SEQ, D_MODEL)),
        'x_sample': nrm((DEC_BATCH, DEC_SEQ, D_MODEL)),
        'mem_prompt': nrm((BATCH, MEM_TOKENS, D_MODEL)),
        'cache_mla_latent': nrm((N_EVEN, n_phys, PAGE_SIZE, MLA_KV_LORA)),
        'cache_mla_krope': nrm((N_EVEN, n_phys, PAGE_SIZE, MLA_ROPE)),
        'state_hgrn': nrm((N_EVEN, DEC_BATCH, HG_HEADS, HG_DK, HG_DV), 0.5),
        'cache_fox_k': nrm((N_ODD, n_phys, PAGE_SIZE, FOX_KV_HEADS, FOX_HD)),
        'cache_fox_v': nrm((N_ODD, n_phys, PAGE_SIZE, FOX_KV_HEADS, FOX_HD)),
        'cache_fox_logf': jax.nn.log_sigmoid(FOX_F_BIAS_INIT + nrm((N_ODD, n_phys, PAGE_SIZE, FOX_HEADS))),
        'cache_mem_k': nrm((DEPTH, DEC_BATCH, MEM_TOKENS, MEM_HEADS, MEM_HD)),
        'cache_mem_v': nrm((DEPTH, DEC_BATCH, MEM_TOKENS, MEM_HEADS, MEM_HD)),
        'page_table': page_table,
        'norm_gains': gain((DEPTH, N_NORMS, D_MODEL)),
        'w_ffn_gate': nrm((DEPTH, 2, D_MODEL, D_FF), D_MODEL ** -0.5),
        'w_ffn_up': nrm((DEPTH, 2, D_MODEL, D_FF), D_MODEL ** -0.5),
        'w_ffn_down': nrm((DEPTH, 2, D_FF, D_MODEL), D_FF ** -0.5),
        'w_mem_q': nrm((DEPTH, D_MODEL, MEM_INNER), D_MODEL ** -0.5),
        'w_mem_k': nrm((DEPTH, D_MODEL, MEM_INNER), D_MODEL ** -0.5),
        'w_mem_v': nrm((DEPTH, D_MODEL, MEM_INNER), D_MODEL ** -0.5),
        'w_mem_o': nrm((DEPTH, MEM_INNER, D_MODEL), MEM_INNER ** -0.5),
        'w_in_ab': nrm((N_EVEN, D_MODEL, D_IN_AB), D_MODEL ** -0.5),
        'g_mla_q': gain((N_EVEN, MLA_Q_LORA)),
        'g_mla_kv': gain((N_EVEN, MLA_KV_LORA)),
        'w_mla_uq': nrm((N_EVEN, MLA_Q_LORA, MLA_HEADS * (MLA_NOPE + MLA_ROPE)), MLA_Q_LORA ** -0.5),
        'w_mla_uk': nrm((N_EVEN, MLA_KV_LORA, MLA_HEADS, MLA_NOPE), MLA_KV_LORA ** -0.5),
        'w_mla_uv': nrm((N_EVEN, MLA_KV_LORA, MLA_HEADS, MLA_V), MLA_KV_LORA ** -0.5),
        'hgrn_lb_logits': nrm((N_EVEN + 1, HG_HEADS * HG_DK), 0.5),
        'g_hgrn_out': gain((N_EVEN, HG_HEADS * HG_DV)),
        'w_out_ab': nrm((N_EVEN, D_MIX_AB, D_MODEL), D_MIX_AB ** -0.5),
        'w_in_c': nrm((N_ODD, D_MODEL, D_IN_C), D_MODEL ** -0.5),
        'b_fox_f': FOX_F_BIAS_INIT + nrm((N_ODD, FOX_HEADS), 0.5),
        'w_out_c': nrm((N_ODD, FOX_HEADS * FOX_HD, D_MODEL), (FOX_HEADS * FOX_HD) ** -0.5),
    }


def reference(x_prompt, x_sample, mem_prompt, cache_mla_latent, cache_mla_krope, state_hgrn, cache_fox_k,
              cache_fox_v, cache_fox_logf, cache_mem_k, cache_mem_v, page_table, norm_gains, w_ffn_gate, w_ffn_up,
              w_ffn_down, w_mem_q, w_mem_k, w_mem_v, w_mem_o, w_in_ab, g_mla_q, g_mla_kv, w_mla_uq, w_mla_uk,
              w_mla_uv, hgrn_lb_logits, g_hgrn_out, w_out_ab, w_in_c, b_fox_f, w_out_c):
    f32 = jnp.float32
    Bp, Tp, _ = x_prompt.shape
    Bs, Ts, _ = x_sample.shape
    past_len = page_table.shape[1] * PAGE_SIZE
    pos_p = jnp.arange(Tp)
    pos_s = past_len + jnp.arange(Ts)
    lb_all = jnp.cumsum(jax.nn.softmax(hgrn_lb_logits.astype(f32), axis=0), axis=0)

    lat_p, kr_p, hg_p, fk_p, fv_p, flf_p, mk_p, mv_p = [], [], [], [], [], [], [], []
    lat_s, kr_s, hg_s, fk_s, fv_s, flf_s = [], [], [], [], [], []
    xp, xs = x_prompt, x_sample
    for l in range(DEPTH):
        ng = norm_gains[l]
        xp = macaron_half(xp, ng[N_FFN1_PRE], ng[N_FFN1_POST], w_ffn_gate[l, 0], w_ffn_up[l, 0], w_ffn_down[l, 0])
        xs = macaron_half(xs, ng[N_FFN1_PRE], ng[N_FFN1_POST], w_ffn_gate[l, 0], w_ffn_up[l, 0], w_ffn_down[l, 0])
        hp = rmsnorm(xp, ng[N_MIX_PRE])
        hs = rmsnorm(xs, ng[N_MIX_PRE])
        if l % 2 == 0:
            e = l // 2
            wts = (w_in_ab[e], g_mla_q[e], g_mla_kv[e], w_mla_uq[e], w_mla_uk[e], w_mla_uv[e], lb_all[e],
                   g_hgrn_out[e], w_out_ab[e])
            s0_p = jnp.zeros((Bp, HG_HEADS, HG_DK, HG_DV), f32)
            mp, (c_p, r_p, S_p) = even_mixer(hp, pos_p, mla_prompt_attend, s0_p, *wts)
            attend_s = functools.partial(mla_sample_attend, lat_pool=cache_mla_latent, kr_pool=cache_mla_krope,
                                         layer=e, page_table=page_table)
            ms, (c_s, r_s, S_s) = even_mixer(hs, pos_s, attend_s, state_hgrn[e], *wts)
            lat_p.append(c_p)
            kr_p.append(r_p)
            hg_p.append(S_p)
            lat_s.append(c_s)
            kr_s.append(r_s)
            hg_s.append(S_s)
        else:
            o = l // 2
            mp, (k_p, v_p, f_p) = odd_mixer(hp, fox_prompt_attend, w_in_c[o], b_fox_f[o], w_out_c[o])
            attend_s = functools.partial(fox_sample_attend, k_pool=cache_fox_k, v_pool=cache_fox_v,
                                         lf_pool=cache_fox_logf, layer=o, page_table=page_table)
            ms, (k_s, v_s, f_s) = odd_mixer(hs, attend_s, w_in_c[o], b_fox_f[o], w_out_c[o])
            fk_p.append(k_p)
            fv_p.append(v_p)
            flf_p.append(f_p)
            fk_s.append(k_s)
            fv_s.append(v_s)
            flf_s.append(f_s)
        xp = xp + rmsnorm(mp, ng[N_MIX_POST])
        xs = xs + rmsnorm(ms, ng[N_MIX_POST])
        mkp, mvp = mem_kv(mem_prompt, ng[N_MEM_SRC], w_mem_k[l], w_mem_v[l])
        mk_p.append(mkp)
        mv_p.append(mvp)
        xp = xp + rmsnorm(mem_attend(rmsnorm(xp, ng[N_MEM_PRE]), mkp, mvp, w_mem_q[l], w_mem_o[l]), ng[N_MEM_POST])
        xs = xs + rmsnorm(mem_attend(rmsnorm(xs, ng[N_MEM_PRE]), cache_mem_k[l], cache_mem_v[l], w_mem_q[l], w_mem_o[l]),
                          ng[N_MEM_POST])
        xp = macaron_half(xp, ng[N_FFN2_PRE], ng[N_FFN2_POST], w_ffn_gate[l, 1], w_ffn_up[l, 1], w_ffn_down[l, 1])
        xs = macaron_half(xs, ng[N_FFN2_PRE], ng[N_FFN2_POST], w_ffn_gate[l, 1], w_ffn_up[l, 1], w_ffn_down[l, 1])

    return (xp, xs,
            jnp.stack(lat_p), jnp.stack(kr_p), jnp.stack(hg_p), jnp.stack(fk_p), jnp.stack(fv_p), jnp.stack(flf_p),
            jnp.stack(mk_p), jnp.stack(mv_p),
            jnp.stack(lat_s), jnp.stack(kr_s), jnp.stack(hg_s), jnp.stack(fk_s), jnp.stack(fv_s), jnp.stack(flf_s))
```

```python
import functools
import math

import jax
import jax.numpy as jnp
import numpy as np
from jax import lax
from jax.experimental import pallas as pl
from jax.experimental.pallas import tpu as pltpu

F32 = jnp.float32
BF16 = jnp.bfloat16

MLA_HEADS = 8
MLA_NOPE = 64
MLA_ROPE = 32
MLA_V = 64
MLA_KV_LORA = 256
MLA_Q_LORA = 384
ROPE_THETA = 10000.0
HG_HEADS = 4
HG_DK = 128
HG_DV = 128
FOX_HEADS = 16
FOX_KV_HEADS = 4
FOX_HD = 64
MEM_HEADS = 4
MEM_HD = 128
PAGE_SIZE = 128
EPS = 1e-6
N_FFN1_PRE, N_FFN1_POST, N_MIX_PRE, N_MIX_POST, N_MEM_PRE, N_MEM_POST, N_FFN2_PRE, N_FFN2_POST, N_MEM_SRC = range(9)

LANES = 128
VMEM_LIMIT = 56 * 1024 * 1024
TOKEN_TILE = 512
INPROJ_TILE = 256
FFN_TILE = 768
FFN_CHUNK = 512
ATTN_BLOCK = 512
HGRN_BLOCK = 256
HGRN_CHUNK = 16
PAGES_PER_STEP = 8
MEM_GROUP = 8


def _params(*sem):
    return pltpu.CompilerParams(dimension_semantics=sem, vmem_limit_bytes=VMEM_LIMIT)


def _rms(x, g):
    return x * lax.rsqrt(jnp.mean(x * x, axis=-1, keepdims=True) + EPS) * g


def _dot(a, b):
    return jnp.dot(a.astype(BF16), b.astype(BF16), preferred_element_type=F32)


def _dot_nt(a, b):
    return lax.dot_general(a.astype(BF16), b.astype(BF16), (((1,), (1,)), ((), ())), preferred_element_type=F32)


def _dot_tn(a, b):
    return lax.dot_general(a.astype(BF16), b.astype(BF16), (((0,), (0,)), ((), ())), preferred_element_type=F32)


def _split3(a):
    hi = a.astype(BF16)
    r = a - hi.astype(F32)
    mid = r.astype(BF16)
    lo = (r - mid.astype(F32)).astype(BF16)
    return hi, mid, lo


def _dot_exact_rhs(m01, a):
    hi, mid, lo = _split3(a)
    return (jnp.dot(m01, hi, preferred_element_type=F32) + jnp.dot(m01, mid, preferred_element_type=F32)
            + jnp.dot(m01, lo, preferred_element_type=F32))


def _dot_exact_lhs(a, m01):
    hi, mid, lo = _split3(a)
    return (jnp.dot(hi, m01, preferred_element_type=F32) + jnp.dot(mid, m01, preferred_element_type=F32)
            + jnp.dot(lo, m01, preferred_element_type=F32))


def _log_sigmoid(x):
    return jnp.minimum(x, 0.0) - jnp.log(1.0 + jnp.exp(-jnp.abs(x)))


def _pick_tile(n, target):
    t = min(n, target)
    while n % t:
        t -= 8
    assert t > 0 and t % 8 == 0, (n, target)
    return t


def _full(shape):
    return pl.BlockSpec(shape, lambda *_: (0,) * len(shape))


def _resident(shape):
    return pl.BlockSpec(shape, lambda *_: (0,) * len(shape), pipeline_mode=pl.Buffered(1))


def _ffn_kernel(x_ref, gpre_ref, gpost_ref, wg_ref, wu_ref, wd_ref, o_ref, acc_ref, *, chunks):
    x = x_ref[...]
    h = _rms(x, gpre_ref[...]).astype(BF16)
    for idx, (c0, cw) in enumerate(chunks):
        g = jnp.dot(h, wg_ref[:, c0:c0 + cw], preferred_element_type=F32)
        u = jnp.dot(h, wu_ref[:, c0:c0 + cw], preferred_element_type=F32)
        a = (g * jax.nn.sigmoid(g) * u).astype(BF16)
        d = jnp.dot(a, wd_ref[c0:c0 + cw, :], preferred_element_type=F32)
        if idx == 0:
            acc_ref[...] = d
        else:
            acc_ref[...] += d
    o_ref[...] = x + 0.5 * _rms(acc_ref[...], gpost_ref[...])


def _ffn_half(x, g_pre, g_post, w_gate, w_up, w_down):
    n, d = x.shape
    f = w_gate.shape[1]
    tm = _pick_tile(n, FFN_TILE)
    chunks = tuple((c0, min(FFN_CHUNK, f - c0)) for c0 in range(0, f, FFN_CHUNK))
    return pl.pallas_call(
        functools.partial(_ffn_kernel, chunks=chunks),
        grid=(n // tm,),
        in_specs=[pl.BlockSpec((tm, d), lambda i: (i, 0)), _full((1, d)), _full((1, d)),
                  _resident((d, f)), _resident((d, f)), _resident((f, d))],
        out_specs=pl.BlockSpec((tm, d), lambda i: (i, 0)),
        out_shape=jax.ShapeDtypeStruct((n, d), F32),
        scratch_shapes=[pltpu.VMEM((tm, d), F32)],
        compiler_params=_params("parallel"),
        name="ffn_half",
    )(x, g_pre.reshape(1, d), g_post.reshape(1, d), w_gate.astype(BF16), w_up.astype(BF16), w_down.astype(BF16))


def _norm_matmul_kernel(x_ref, g_ref, *refs, scales):
    n_out = len(scales)
    h = _rms(x_ref[...], g_ref[...]).astype(BF16)
    for w_ref, o_ref, s in zip(refs[:n_out], refs[n_out:], scales):
        y = jnp.dot(h, w_ref[...], preferred_element_type=F32)
        o_ref[...] = (y if s == 1.0 else y * s).astype(o_ref.dtype)


def _norm_matmul(x, g, ws, scales, dtypes):
    n, d = x.shape
    tm = _pick_tile(n, TOKEN_TILE)
    return pl.pallas_call(
        functools.partial(_norm_matmul_kernel, scales=tuple(scales)),
        grid=(n // tm,),
        in_specs=[pl.BlockSpec((tm, d), lambda i: (i, 0)), _full((1, d))] + [_resident(w.shape) for w in ws],
        out_specs=[pl.BlockSpec((tm, w.shape[1]), lambda i: (i, 0)) for w in ws],
        out_shape=[jax.ShapeDtypeStruct((n, w.shape[1]), dt) for w, dt in zip(ws, dtypes)],
        compiler_params=_params("parallel"),
        name="norm_matmul",
    )(x, g.reshape(1, d), *[w.astype(BF16) for w in ws])


def _outproj_kernel(x_ref, g_ref, *refs, n_parts):
    acc = None
    for a_ref, w_ref in zip(refs[:n_parts], refs[n_parts:2 * n_parts]):
        y = jnp.dot(a_ref[...].astype(BF16), w_ref[...], preferred_element_type=F32)
        acc = y if acc is None else acc + y
    refs[2 * n_parts][...] = x_ref[...] + _rms(acc, g_ref[...])


def _outproj_residual(x, g, parts, ws, row0):
    n, d = x.shape
    rows = parts[0].shape[0]
    tm = _pick_tile(rows, TOKEN_TILE)
    assert row0 % tm == 0
    off = row0 // tm
    return pl.pallas_call(
        functools.partial(_outproj_kernel, n_parts=len(parts)),
        grid=(rows // tm,),
        in_specs=[pl.BlockSpec((tm, d), lambda i: (i + off, 0)), _full((1, d))]
        + [pl.BlockSpec((tm, p.shape[1]), lambda i: (i, 0)) for p in parts] + [_resident(w.shape) for w in ws],
        out_specs=pl.BlockSpec((tm, d), lambda i: (i + off, 0)),
        out_shape=jax.ShapeDtypeStruct((n, d), F32),
        input_output_aliases={0: 0},
        compiler_params=_params("parallel"),
        name="outproj_residual",
    )(x, g.reshape(1, d), *parts, *[w.astype(BF16) for w in ws])


def _even_inproj_kernel(x_ref, gpre_ref, cos_ref, sin_ref, wq_ref, wc_ref, wkr_ref, wh_ref, wuq_ref, wuk_ref, wuv_ref,
                        gq_ref, gkv_ref, lb_ref,
                        q_ref, kp_ref, vp_ref, c_ref, krp_ref, hq_ref, logf_ref, kk_ref, hi_ref, gate_ref, *, q_scale):
    h = _rms(x_ref[...], gpre_ref[...]).astype(BF16)
    cos_t = cos_ref[...]
    sin_t = sin_ref[...]
    nh = q_ref.shape[0]
    qn = _rms(jnp.dot(h, wq_ref[...], preferred_element_type=F32), gq_ref[...]).astype(BF16)
    q2 = jnp.dot(qn, wuq_ref[...], preferred_element_type=F32)
    for i in range(nh):
        a = q2[:, i * LANES:(i + 1) * LANES]
        b = q2[:, (nh + i) * LANES:(nh + i + 1) * LANES]
        q_ref[i] = ((a * cos_t + b * sin_t) * q_scale).astype(q_ref.dtype)
    c = _rms(jnp.dot(h, wc_ref[...], preferred_element_type=F32), gkv_ref[...])
    c_ref[...] = c
    kr2 = jnp.dot(h, wkr_ref[...], preferred_element_type=F32)
    krp = kr2[:, :LANES] * cos_t + kr2[:, LANES:] * sin_t
    krp_ref[...] = krp
    cb = c.astype(BF16)
    kn = jnp.dot(cb, wuk_ref[...], preferred_element_type=F32)
    vv = jnp.dot(cb, wuv_ref[...], preferred_element_type=F32)
    for i in range(nh):
        kp_ref[i] = (kn[:, i * LANES:(i + 1) * LANES] + krp).astype(kp_ref.dtype)
        vp_ref[i] = vv[:, i * LANES:(i + 1) * LANES].astype(vp_ref.dtype)
    hh = jnp.dot(h, wh_ref[...], preferred_element_type=F32)
    w = hq_ref.shape[1]
    lb = lb_ref[...]
    z = hh[:, w:2 * w]
    hg = hh[:, 3 * w:]
    hq_ref[...] = hh[:, :w]
    logf_ref[...] = jnp.log(lb + (1.0 - lb) * jax.nn.sigmoid(z))
    kk_ref[...] = (1.0 - lb) * jax.nn.sigmoid(-z)
    hi_ref[...] = hh[:, 2 * w:3 * w]
    gate_ref[...] = hg * jax.nn.sigmoid(hg)


def _rot_cols(w):
    half = w.shape[-1] // 2
    return jnp.concatenate([-w[..., half:], w[..., :half]], axis=-1)


def _even_inproj(x, g_pre, cos_t, sin_t, w_in, g_q, g_kv, w_uq, w_uk, w_uv, lb, n_prompt):
    n, d = x.shape
    tm = _pick_tile(math.gcd(n_prompt, n - n_prompt), INPROJ_TILE)
    nh, hw = MLA_HEADS, HG_HEADS * HG_DK
    o_q, o_c, o_kr = MLA_Q_LORA, MLA_Q_LORA + MLA_KV_LORA, MLA_Q_LORA + MLA_KV_LORA + MLA_ROPE
    wq, wc, wkr, wh = w_in[:, :o_q], w_in[:, o_q:o_c], w_in[:, o_c:o_kr], w_in[:, o_kr:]
    wkr2 = jnp.zeros((d, 2 * LANES), F32)
    wkr2 = wkr2.at[:, MLA_NOPE:MLA_NOPE + MLA_ROPE].set(wkr).at[:, LANES + MLA_NOPE:LANES + MLA_NOPE + MLA_ROPE].set(_rot_cols(wkr))
    uq = w_uq.reshape(MLA_Q_LORA, nh, MLA_NOPE + MLA_ROPE)
    wuq2 = jnp.zeros((MLA_Q_LORA, 2, nh, LANES), F32)
    wuq2 = wuq2.at[:, 0, :, :MLA_NOPE + MLA_ROPE].set(uq)
    wuq2 = wuq2.at[:, 1, :, MLA_NOPE:MLA_NOPE + MLA_ROPE].set(_rot_cols(uq[..., MLA_NOPE:]))
    wuq2 = wuq2.reshape(MLA_Q_LORA, 2 * nh * LANES)
    wuk = jnp.zeros((MLA_KV_LORA, nh, LANES), F32).at[:, :, :MLA_NOPE].set(w_uk).reshape(MLA_KV_LORA, nh * LANES)
    wuv = jnp.zeros((MLA_KV_LORA, nh // 2, 2, 2, MLA_V), F32)
    uv = w_uv.reshape(MLA_KV_LORA, nh // 2, 2, MLA_V)
    wuv = wuv.at[:, :, 0, 0].set(uv[:, :, 0]).at[:, :, 1, 1].set(uv[:, :, 1]).reshape(MLA_KV_LORA, nh * LANES)
    ws = [w.astype(BF16) for w in (wq, wc, wkr2, wh, wuq2, wuk, wuv)]
    tok = lambda w: pl.BlockSpec((tm, w), lambda i: (i, 0))
    head = pl.BlockSpec((nh, tm, LANES), lambda i: (0, i, 0))
    outs = pl.pallas_call(
        functools.partial(_even_inproj_kernel, q_scale=float((MLA_NOPE + MLA_ROPE) ** -0.5)),
        grid=(n // tm,),
        in_specs=[tok(d), _full((1, d)), tok(LANES), tok(LANES)] + [_resident(w.shape) for w in ws]
        + [_full((1, MLA_Q_LORA)), _full((1, MLA_KV_LORA)), _full((1, hw))],
        out_specs=[head, head, head, tok(MLA_KV_LORA), tok(LANES)] + [tok(hw)] * 5,
        out_shape=[jax.ShapeDtypeStruct((nh, n, LANES), BF16)] * 3
        + [jax.ShapeDtypeStruct((n, MLA_KV_LORA), F32), jax.ShapeDtypeStruct((n, LANES), F32)]
        + [jax.ShapeDtypeStruct((n, hw), F32)] * 5,
        compiler_params=_params("parallel"),
        name="even_inproj",
    )(x, g_pre.reshape(1, d), cos_t, sin_t, *ws, g_q.reshape(1, -1), g_kv.reshape(1, -1), lb.reshape(1, -1))
    return outs


def _odd_inproj_kernel(x_ref, gpre_ref, wq_ref, wk_ref, wve_ref, wvo_ref, wkf_ref, wvf_ref, wf_ref, bf_ref,
                       selq_ref, selk_ref, cq_ref, ck_ref,
                       qa_ref, ka_ref, v2_ref, kf_ref, vf_ref, logf_ref, cum_ref, carry_ref,
                       *, n_prompt_tiles, tiles_per_seq, sample_shift):
    i = pl.program_id(0)
    tm = x_ref.shape[0]
    is_sample = i >= n_prompt_tiles

    @pl.when(i == 0)
    def _():
        carry_ref[...] = jnp.zeros(carry_ref.shape, F32)

    h = _rms(x_ref[...], gpre_ref[...]).astype(BF16)
    lane = lax.broadcasted_iota(jnp.int32, (tm, LANES), 1)
    fz = jnp.dot(h, wf_ref[...], preferred_element_type=F32) + bf_ref[...]
    logf = jnp.where(lane < FOX_HEADS, _log_sigmoid(fz), 0.0)
    logf_ref[...] = logf
    row = lax.broadcasted_iota(jnp.int32, (tm, tm), 0)
    col = lax.broadcasted_iota(jnp.int32, (tm, tm), 1)
    shift = jnp.where(is_sample, sample_shift, 30)
    tri = jnp.where((col <= row) & ((row >> shift) == (col >> shift)), 1.0, 0.0).astype(BF16)
    fresh = jnp.logical_or(is_sample, i % tiles_per_seq == 0)
    carry = jnp.where(fresh, 0.0, carry_ref[...])
    cum = _dot_exact_rhs(tri, logf) + carry
    cum_ref[...] = cum
    carry_ref[...] = cum[tm - 1:tm, :]
    hi, mid, lo = _split3(cum)
    qa = (jnp.dot(h, wq_ref[...], preferred_element_type=F32) + jnp.dot(hi, selq_ref[0], preferred_element_type=F32)
          + jnp.dot(mid, selq_ref[1], preferred_element_type=F32) + jnp.dot(lo, selq_ref[2], preferred_element_type=F32)
          + cq_ref[...])
    for j in range(qa_ref.shape[0]):
        qa_ref[j] = qa[:, j * LANES:(j + 1) * LANES].astype(qa_ref.dtype)
    ka = (jnp.dot(h, wk_ref[...], preferred_element_type=F32) + jnp.dot(hi, selk_ref[0], preferred_element_type=F32)
          + jnp.dot(mid, selk_ref[1], preferred_element_type=F32) + jnp.dot(lo, selk_ref[2], preferred_element_type=F32)
          + ck_ref[...])
    ve = jnp.dot(h, wve_ref[...], preferred_element_type=F32)
    vo = jnp.dot(h, wvo_ref[...], preferred_element_type=F32)
    for j in range(ka_ref.shape[0]):
        ka_ref[j] = ka[:, j * LANES:(j + 1) * LANES].astype(ka_ref.dtype)
        v2_ref[2 * j] = ve[:, j * LANES:(j + 1) * LANES].astype(v2_ref.dtype)
        v2_ref[2 * j + 1] = vo[:, j * LANES:(j + 1) * LANES].astype(v2_ref.dtype)
    kf_ref[...] = jnp.dot(h, wkf_ref[...], preferred_element_type=F32)
    vf_ref[...] = jnp.dot(h, wvf_ref[...], preferred_element_type=F32)


def _fox_selectors():
    nq, nk, g = FOX_HEADS, FOX_KV_HEADS, FOX_HEADS // FOX_KV_HEADS
    selq = np.zeros((3, LANES, nq * LANES), np.float32)
    selk = np.zeros((3, LANES, nk * LANES), np.float32)
    cq = np.zeros((1, nq * LANES), np.float32)
    ck = np.zeros((1, nk * LANES), np.float32)
    base = FOX_HD
    for hd in range(nq):
        n, gi = divmod(hd, g)
        for t in range(3):
            selq[t, hd, hd * LANES + base + t] = 1.0
            cq[0, hd * LANES + base + 3 + 3 * gi + t] = 1.0
            selk[t, hd, n * LANES + base + 3 + 3 * gi + t] = -1.0
    for n in range(nk):
        ck[0, n * LANES + base:n * LANES + base + 3] = 1.0
    return jnp.asarray(selq, BF16), jnp.asarray(selk, BF16), jnp.asarray(cq), jnp.asarray(ck)


def _odd_inproj(x, g_pre, w_in, b_f, n_prompt, seq_prompt, seq_sample):
    n, d = x.shape
    tm = _pick_tile(math.gcd(math.gcd(n_prompt, n - n_prompt), seq_prompt), INPROJ_TILE)
    assert n_prompt % tm == 0 and seq_prompt % tm == 0 and tm % seq_sample == 0
    assert seq_sample & (seq_sample - 1) == 0, "sample sequences must be a power of two long"
    nq, nk, hd = FOX_HEADS, FOX_KV_HEADS, FOX_HD
    o_q, o_k, o_v = nq * hd, nq * hd + nk * hd, nq * hd + 2 * nk * hd
    w_q, w_k, w_v, w_f = w_in[:, :o_q], w_in[:, o_q:o_k], w_in[:, o_k:o_v], w_in[:, o_v:]
    wq = jnp.zeros((d, nq, LANES), F32).at[:, :, :hd].set(w_q.reshape(d, nq, hd) * float(hd ** -0.5)).reshape(d, nq * LANES)
    wk = jnp.zeros((d, nk, LANES), F32).at[:, :, :hd].set(w_k.reshape(d, nk, hd)).reshape(d, nk * LANES)
    wve = jnp.zeros((d, nk, LANES), F32).at[:, :, :hd].set(w_v.reshape(d, nk, hd)).reshape(d, nk * LANES)
    wvo = jnp.zeros((d, nk, LANES), F32).at[:, :, hd:].set(w_v.reshape(d, nk, hd)).reshape(d, nk * LANES)
    wf = jnp.zeros((d, LANES), F32).at[:, :nq].set(w_f)
    bf = jnp.zeros((1, LANES), F32).at[0, :nq].set(b_f)
    selq, selk, cq, ck = _fox_selectors()
    ws = [w.astype(BF16) for w in (wq, wk, wve, wvo, w_k, w_v, wf)]
    tok = lambda w: pl.BlockSpec((tm, w), lambda i: (i, 0))
    head = lambda k: pl.BlockSpec((k, tm, LANES), lambda i: (0, i, 0))
    return pl.pallas_call(
        functools.partial(_odd_inproj_kernel, n_prompt_tiles=n_prompt // tm, tiles_per_seq=seq_prompt // tm,
                          sample_shift=int(seq_sample).bit_length() - 1),
        grid=(n // tm,),
        in_specs=[tok(d), _full((1, d))] + [_resident(w.shape) for w in ws]
        + [_full((1, LANES)), _resident(selq.shape), _resident(selk.shape), _full(cq.shape), _full(ck.shape)],
        out_specs=[head(nq), head(nk), head(2 * nk), tok(nk * hd), tok(nk * hd), tok(LANES), tok(LANES)],
        out_shape=[jax.ShapeDtypeStruct((nq, n, LANES), BF16), jax.ShapeDtypeStruct((nk, n, LANES), BF16),
                   jax.ShapeDtypeStruct((2 * nk, n, LANES), BF16), jax.ShapeDtypeStruct((n, nk * hd), F32),
                   jax.ShapeDtypeStruct((n, nk * hd), F32), jax.ShapeDtypeStruct((n, LANES), F32),
                   jax.ShapeDtypeStruct((n, LANES), F32)],
        scratch_shapes=[pltpu.VMEM((1, LANES), F32)],
        compiler_params=_params("arbitrary"),
        name="odd_inproj",
    )(x, g_pre.reshape(1, d), *ws, bf, selq, selk, cq, ck)


def _flash_kernel(qi_ref, kj_ref, q_ref, k_ref, v_ref, o_ref, m_ref, l_ref, acc_ref, *, hps, shared_k):
    p = pl.program_id(2)
    i = qi_ref[p]
    j = kj_ref[p]
    tq, tk = q_ref.shape[1], k_ref.shape[1]

    @pl.when(j == 0)
    def _():
        m_ref[...] = jnp.full(m_ref.shape, -jnp.inf, F32)
        l_ref[...] = jnp.zeros(l_ref.shape, F32)
        acc_ref[...] = jnp.zeros(acc_ref.shape, F32)

    def update(masked):
        if masked:
            keep = lax.broadcasted_iota(jnp.int32, (tq, tk), 1) <= lax.broadcasted_iota(jnp.int32, (tq, tk), 0)
        for hh in range(hps):
            s = _dot_nt(q_ref[hh], k_ref[0 if shared_k else hh])
            if masked:
                s = jnp.where(keep, s, -jnp.inf)
            m_prev = m_ref[hh]
            m_new = jnp.maximum(m_prev, jnp.max(s, axis=-1, keepdims=True))
            alpha = jnp.exp(m_prev - m_new)
            pe = jnp.exp(s - m_new)
            l_ref[hh] = alpha * l_ref[hh] + jnp.sum(pe, axis=-1, keepdims=True)
            acc_ref[hh] = alpha * acc_ref[hh] + jnp.dot(pe.astype(BF16), v_ref[hh % 2], preferred_element_type=F32)
            m_ref[hh] = m_new

    @pl.when(j < i)
    def _():
        update(False)

    @pl.when(j == i)
    def _():
        update(True)
        for r in range(hps // 2):
            o = acc_ref[2 * r] / l_ref[2 * r] + acc_ref[2 * r + 1] / l_ref[2 * r + 1]
            o_ref[:, r * LANES:(r + 1) * LANES] = o.astype(o_ref.dtype)


def _flash_causal(q, k, v, n_seq, seq_len, hps, shared_k):
    nh = q.shape[0]
    groups = nh // hps
    blk = _pick_tile(seq_len, ATTN_BLOCK)
    nb = seq_len // blk
    pairs = [(a, b) for a in range(nb) for b in range(a + 1)]
    qi = jnp.asarray([a for a, _ in pairs], jnp.int32)
    kj = jnp.asarray([b for _, b in pairs], jnp.int32)
    kh = 1 if shared_k else hps
    gs = pltpu.PrefetchScalarGridSpec(
        num_scalar_prefetch=2, grid=(n_seq, groups, len(pairs)),
        in_specs=[pl.BlockSpec((hps, blk, LANES), lambda b, g, p, qi, kj: (g, b * nb + qi[p], 0)),
                  pl.BlockSpec((kh, blk, LANES), lambda b, g, p, qi, kj: (g, b * nb + kj[p], 0)),
                  pl.BlockSpec((2, blk, LANES), lambda b, g, p, qi, kj: (g, b * nb + kj[p], 0))],
        out_specs=pl.BlockSpec((blk, hps * LANES // 2), lambda b, g, p, qi, kj: (b * nb + qi[p], g)),
        scratch_shapes=[pltpu.VMEM((hps, blk, 1), F32), pltpu.VMEM((hps, blk, 1), F32), pltpu.VMEM((hps, blk, LANES), F32)])
    return pl.pallas_call(
        functools.partial(_flash_kernel, hps=hps, shared_k=shared_k),
        grid_spec=gs,
        out_shape=jax.ShapeDtypeStruct((n_seq * seq_len, nh * LANES // 2), BF16),
        compiler_params=_params("parallel", "parallel", "arbitrary"),
        name="flash_causal",
    )(qi, kj, q, k, v)


def _hgrn_kernel(hq_ref, kk_ref, hi_ref, logf_ref, gate_ref, s0_ref, gout_ref, o_ref, s_ref, obuf_ref, *, chunk):
    j = pl.program_id(1)
    tb = hq_ref.shape[0]
    nh = s_ref.shape[1]
    dk = s_ref.shape[2]

    @pl.when(j == 0)
    def _():
        s_ref[...] = s0_ref[...]

    row = lax.broadcasted_iota(jnp.int32, (chunk, chunk), 0)
    col = lax.broadcasted_iota(jnp.int32, (chunk, chunk), 1)
    causal = col <= row
    tri = jnp.where(causal, 1.0, 0.0).astype(BF16)

    def step(c, carry):
        r0 = pl.multiple_of(c * chunk, chunk)
        for hh in range(nh):
            sl = (pl.ds(r0, chunk), slice(hh * dk, (hh + 1) * dk))
            q, k, v, g = hq_ref[sl], kk_ref[sl], hi_ref[sl], logf_ref[sl]
            cum = _dot_exact_rhs(tri, g)
            a = jnp.zeros((chunk, chunk), F32)
            for s in range(chunk):
                w = q * k[s:s + 1, :] * jnp.exp(jnp.minimum(cum - cum[s:s + 1, :], 0.0))
                a = jnp.where(col == s, jnp.sum(w, axis=-1, keepdims=True), a)
            a = jnp.where(causal, a, 0.0)
            state = s_ref[0, hh]
            o = _dot(a, v) + _dot(q * jnp.exp(cum), state)
            obuf_ref[sl] = o
            last = cum[chunk - 1:chunk, :]
            decay = jnp.broadcast_to(jnp.exp(last), (8, dk)).T[:, 0:1]
            s_ref[0, hh] = decay * state + _dot_tn(k * jnp.exp(last - cum), v)
        return carry

    lax.fori_loop(0, tb // chunk, step, 0)
    for hh in range(nh):
        sl = (slice(None), slice(hh * dk, (hh + 1) * dk))
        o_ref[sl] = (_rms(obuf_ref[sl], gout_ref[sl]) * gate_ref[sl]).astype(o_ref.dtype)


def _hgrn(hq, kk, hi, logf, gate, s0, g_out, n_seq, seq_len, block, chunk, dtype):
    w = hq.shape[1]
    nb = seq_len // block
    tok = pl.BlockSpec((block, w), lambda s, j: (s * nb + j, 0))
    st = pl.BlockSpec((1,) + s0.shape[1:], lambda s, j: (s, 0, 0, 0))
    return pl.pallas_call(
        functools.partial(_hgrn_kernel, chunk=chunk),
        grid=(n_seq, nb),
        in_specs=[tok] * 5 + [st, _full((1, w))],
        out_specs=[tok, st],
        out_shape=[jax.ShapeDtypeStruct((n_seq * seq_len, w), dtype), jax.ShapeDtypeStruct(s0.shape, F32)],
        scratch_shapes=[pltpu.VMEM((block, w), F32)],
        compiler_params=_params("parallel", "arbitrary"),
        name="hgrn2",
    )(hq, kk, hi, logf, gate, s0, g_out.reshape(1, w))


def _headmm_kernel(x_ref, w_ref, o_ref):
    o_ref[...] = jnp.dot(x_ref[...].astype(BF16), w_ref[...], preferred_element_type=F32).astype(o_ref.dtype)


def _headmm(x, w, rows, row0, dtype):
    nh, _, kdim = x.shape
    assert row0 % rows == 0
    off = row0 // rows
    return pl.pallas_call(
        _headmm_kernel,
        grid=(nh,),
        in_specs=[pl.BlockSpec((None, rows, kdim), lambda h: (h, off, 0)),
                  pl.BlockSpec((None,) + w.shape[1:], lambda h: (h, 0, 0))],
        out_specs=pl.BlockSpec((None, rows, w.shape[2]), lambda h: (h, 0, 0)),
        out_shape=jax.ShapeDtypeStruct((nh, rows, w.shape[2]), dtype),
        compiler_params=_params("parallel"),
        name="headmm",
    )(x, w.astype(BF16))


def _online_update(m_ref, l_ref, acc_ref, s, pv):
    m_prev = m_ref[...]
    m_new = jnp.maximum(m_prev, jnp.max(s, axis=-1, keepdims=True))
    alpha = jnp.exp(m_prev - m_new)
    pe = jnp.exp(s - m_new)
    l_ref[...] = alpha * l_ref[...] + jnp.sum(pe, axis=-1, keepdims=True)
    acc_ref[...] = alpha * acc_ref[...] + pv(pe.astype(BF16))
    m_ref[...] = m_new


def _new_token_mask(rows, t_new):
    r = lax.broadcasted_iota(jnp.int32, (rows, LANES), 0)
    c = lax.broadcasted_iota(jnp.int32, (rows, LANES), 1)
    return (c < t_new) & (c <= (r & (t_new - 1)))


def _mla_paged_kernel(pt_ref, qa_ref, qr_ref, cnew_ref, krnew_ref, *refs, pps, t_new):
    ck_refs, kr_refs = refs[:pps], refs[pps:2 * pps]
    o_ref, m_ref, l_ref, acc_ref = refs[2 * pps:]
    j = pl.program_id(1)
    rows = qa_ref.shape[0]

    @pl.when(j == 0)
    def _():
        m_ref[...] = jnp.full(m_ref.shape, -jnp.inf, F32)
        l_ref[...] = jnp.zeros(l_ref.shape, F32)
        acc_ref[...] = jnp.zeros(acc_ref.shape, F32)

    qa = qa_ref[...]
    qr = qr_ref[...]
    cks = [r[...].astype(BF16) for r in ck_refs]
    s = jnp.concatenate([_dot_nt(qa, ck) + _dot(qr, kr[...]) for ck, kr in zip(cks, kr_refs)], axis=-1)

    def pv(p):
        out = None
        for idx, ck in enumerate(cks):
            y = jnp.dot(p[:, idx * PAGE_SIZE:(idx + 1) * PAGE_SIZE], ck, preferred_element_type=F32)
            out = y if out is None else out + y
        return out

    _online_update(m_ref, l_ref, acc_ref, s, pv)

    @pl.when(j == pl.num_programs(1) - 1)
    def _():
        cn = cnew_ref[...].astype(BF16)
        sn = _dot_nt(qa, cn) + _dot(qr, krnew_ref[...])
        sn = jnp.where(_new_token_mask(rows, t_new), sn, -jnp.inf)
        _online_update(m_ref, l_ref, acc_ref, sn, lambda p: jnp.dot(p, cn, preferred_element_type=F32))
        o_ref[...] = (acc_ref[...] / l_ref[...]).astype(o_ref.dtype)


def _mla_paged(page_table, qa, qr, c_new, kr_new_t, lat_pool, kr_pool_t, layer, t_new):
    bsz, rows, cdim = qa.shape
    n_pages = page_table.shape[1]
    pps = min(PAGES_PER_STEP, n_pages)
    assert n_pages % pps == 0 and t_new & (t_new - 1) == 0
    rdim = qr.shape[2]

    def page(i):
        return lambda b, j, pt: (layer, pt[b, j * pps + i], 0, 0)

    per_b = lambda shape: pl.BlockSpec((None,) + shape, lambda b, j, pt: (b, 0, 0))
    gs = pltpu.PrefetchScalarGridSpec(
        num_scalar_prefetch=1, grid=(bsz, n_pages // pps),
        in_specs=[per_b((rows, cdim)), per_b((rows, rdim)), per_b((PAGE_SIZE, cdim)), per_b((rdim, PAGE_SIZE))]
        + [pl.BlockSpec((None, None, PAGE_SIZE, cdim), page(i)) for i in range(pps)]
        + [pl.BlockSpec((None, None, rdim, PAGE_SIZE), page(i)) for i in range(pps)],
        out_specs=per_b((rows, cdim)),
        scratch_shapes=[pltpu.VMEM((rows, 1), F32), pltpu.VMEM((rows, 1), F32), pltpu.VMEM((rows, cdim), F32)])
    return pl.pallas_call(
        functools.partial(_mla_paged_kernel, pps=pps, t_new=t_new),
        grid_spec=gs,
        out_shape=jax.ShapeDtypeStruct((bsz, rows, cdim), F32),
        compiler_params=_params("parallel", "arbitrary"),
        name="mla_paged",
    )(page_table, qa, qr, c_new, kr_new_t, *([lat_pool] * pps), *([kr_pool_t] * pps))


def _fox_paged_kernel(pt_ref, q_ref, cq_ref, ncq_ref, e_ref, knew_ref, vnew_ref, *refs, pps, t_new, n_kv):
    k_refs, v_refs, lf_refs = refs[:pps], refs[pps:2 * pps], refs[2 * pps:3 * pps]
    o_ref, m_ref, l_ref, acc_ref, suf_ref = refs[3 * pps:]
    j = pl.program_id(1)
    rows = q_ref.shape[0]

    @pl.when(j == 0)
    def _():
        m_ref[...] = jnp.full(m_ref.shape, -jnp.inf, F32)
        l_ref[...] = jnp.zeros(l_ref.shape, F32)
        acc_ref[...] = jnp.zeros(acc_ref.shape, F32)
        suf_ref[...] = jnp.zeros(suf_ref.shape, F32)

    q = q_ref[...]
    cq = cq_ref[...]
    e = e_ref[...]
    r_i = lax.broadcasted_iota(jnp.int32, (PAGE_SIZE, PAGE_SIZE), 0)
    c_i = lax.broadcasted_iota(jnp.int32, (PAGE_SIZE, PAGE_SIZE), 1)
    later = jnp.where(r_i > c_i, 1.0, 0.0).astype(BF16)
    run = suf_ref[...]
    ss = []
    for k_ref, lf_ref in zip(k_refs, lf_refs):
        lf = _dot_exact_rhs(e, lf_ref[...])
        bias = _dot_exact_lhs(lf, later) + (run + cq)
        run = run + jnp.sum(lf, axis=-1, keepdims=True)
        ss.append(_dot(q, k_ref[...]) + bias)
    suf_ref[...] = run
    s = jnp.concatenate(ss, axis=-1)

    def pv(p):
        out = None
        for idx, v_ref in enumerate(v_refs):
            y = _dot_nt(p[:, idx * PAGE_SIZE:(idx + 1) * PAGE_SIZE], v_ref[...])
            out = y if out is None else out + y
        return out

    _online_update(m_ref, l_ref, acc_ref, s, pv)

    @pl.when(j == pl.num_programs(1) - 1)
    def _():
        sn = _dot(q, knew_ref[...]) + cq + ncq_ref[...]
        sn = jnp.where(_new_token_mask(rows, t_new), sn, -jnp.inf)
        vn = vnew_ref[...]
        _online_update(m_ref, l_ref, acc_ref, sn, lambda p: _dot_nt(p, vn))
        rk = rows // n_kv
        hd = acc_ref.shape[1] // n_kv
        for n in range(n_kv):
            o_ref[n * rk:(n + 1) * rk, :] = (acc_ref[n * rk:(n + 1) * rk, n * hd:(n + 1) * hd]
                                             / l_ref[n * rk:(n + 1) * rk, :]).astype(o_ref.dtype)


def _fox_paged(page_table, q_bd, cq, neg_cq, expand, k_new_t, v_new_t, k_pool_t, v_pool_t, lf_pool_t, layer, t_new):
    bsz, rows, kd = q_bd.shape
    n_pages = page_table.shape[1]
    nh = lf_pool_t.shape[2]
    pps = min(PAGES_PER_STEP, n_pages)
    assert n_pages % pps == 0 and t_new & (t_new - 1) == 0

    def page(i):
        return lambda b, j, pt: (layer, pt[b, n_pages - 1 - (j * pps + i)], 0, 0)

    per_b = lambda shape: pl.BlockSpec((None,) + shape, lambda b, j, pt: (b, 0, 0))
    gs = pltpu.PrefetchScalarGridSpec(
        num_scalar_prefetch=1, grid=(bsz, n_pages // pps),
        in_specs=[per_b((rows, kd)), per_b((rows, 1)), per_b((rows, LANES)),
                  pl.BlockSpec((rows, nh), lambda b, j, pt: (0, 0)), per_b((kd, PAGE_SIZE)), per_b((kd, PAGE_SIZE))]
        + [pl.BlockSpec((None, None, kd, PAGE_SIZE), page(i)) for i in range(pps)] * 2
        + [pl.BlockSpec((None, None, nh, PAGE_SIZE), page(i)) for i in range(pps)],
        out_specs=per_b((rows, kd // FOX_KV_HEADS)),
        scratch_shapes=[pltpu.VMEM((rows, 1), F32), pltpu.VMEM((rows, 1), F32), pltpu.VMEM((rows, kd), F32),
                        pltpu.VMEM((rows, 1), F32)])
    return pl.pallas_call(
        functools.partial(_fox_paged_kernel, pps=pps, t_new=t_new, n_kv=FOX_KV_HEADS),
        grid_spec=gs,
        out_shape=jax.ShapeDtypeStruct((bsz, rows, kd // FOX_KV_HEADS), F32),
        compiler_params=_params("parallel", "arbitrary"),
        name="fox_paged",
    )(page_table, q_bd, cq, neg_cq, expand, k_new_t, v_new_t, *([k_pool_t] * pps), *([v_pool_t] * pps),
      *([lf_pool_t] * pps))


def _mem_attn_kernel(q_ref, k_ref, v_ref, o_ref, *, n_heads):
    hd = q_ref.shape[2] // n_heads
    for b in range(q_ref.shape[0]):
        for hh in range(n_heads):
            sl = slice(hh * hd, (hh + 1) * hd)
            s = _dot_nt(q_ref[b, :, sl], k_ref[b, :, sl])
            pe = jnp.exp(s - jnp.max(s, axis=-1, keepdims=True))
            p = (pe / jnp.sum(pe, axis=-1, keepdims=True)).astype(BF16)
            o_ref[b, :, sl] = _dot(p, v_ref[b, :, sl]).astype(o_ref.dtype)


def _mem_attn(q, mk, mv, group, tq, dtype):
    nb, t, w = q.shape
    m = mk.shape[1]
    return pl.pallas_call(
        functools.partial(_mem_attn_kernel, n_heads=MEM_HEADS),
        grid=(nb // group, t // tq),
        in_specs=[pl.BlockSpec((group, tq, w), lambda g, i: (g, i, 0)),
                  pl.BlockSpec((group, m, w), lambda g, i: (g, 0, 0)),
                  pl.BlockSpec((group, m, w), lambda g, i: (g, 0, 0))],
        out_specs=pl.BlockSpec((group, tq, w), lambda g, i: (g, i, 0)),
        out_shape=jax.ShapeDtypeStruct((nb, t, w), dtype),
        compiler_params=_params("parallel", "parallel"),
        name="mem_attn",
    )(q, mk, mv)


def _rope_tables(pos):
    half = MLA_ROPE // 2
    inv_freq = ROPE_THETA ** (-jnp.arange(half, dtype=F32) / half)
    ang = pos.astype(F32)[:, None] * inv_freq[None, :]
    cos, sin = jnp.cos(ang), jnp.sin(ang)
    n = pos.shape[0]
    pad = jnp.zeros((n, LANES - MLA_NOPE - MLA_ROPE), F32)
    cos_t = jnp.concatenate([jnp.ones((n, MLA_NOPE), F32), cos, cos, pad], axis=-1)
    sin_t = jnp.concatenate([jnp.zeros((n, MLA_NOPE), F32), sin, sin, pad], axis=-1)
    return cos_t, sin_t


def _pad_axis(a, axis, size):
    pad = [(0, 0)] * a.ndim
    pad[axis] = (0, size - a.shape[axis])
    return jnp.pad(a, pad)


def kernel(x_prompt, x_sample, mem_prompt, cache_mla_latent, cache_mla_krope, state_hgrn, cache_fox_k, cache_fox_v, cache_fox_logf, cache_mem_k, cache_mem_v, page_table, norm_gains, w_ffn_gate, w_ffn_up, w_ffn_down, w_mem_q, w_mem_k, w_mem_v, w_mem_o, w_in_ab, g_mla_q, g_mla_kv, w_mla_uq, w_mla_uk, w_mla_uv, hgrn_lb_logits, g_hgrn_out, w_out_ab, w_in_c, b_fox_f, w_out_c):
    bp, tp, d = x_prompt.shape
    bs, ts, _ = x_sample.shape
    n_p, n_s = bp * tp, bs * ts
    depth = norm_gains.shape[0]
    n_pages = page_table.shape[1]
    m_tok = mem_prompt.shape[1]
    mem_w = MEM_HEADS * MEM_HD
    hg_w = HG_HEADS * HG_DK
    assert ts <= 8 and ts & (ts - 1) == 0

    x = jnp.concatenate([x_prompt.reshape(n_p, d), x_sample.reshape(n_s, d)], axis=0)
    pos = jnp.concatenate([jnp.tile(jnp.arange(tp), bp), n_pages * PAGE_SIZE + jnp.tile(jnp.arange(ts), bs)])
    cos_t, sin_t = _rope_tables(pos)
    lb_all = jnp.cumsum(jax.nn.softmax(hgrn_lb_logits.astype(F32), axis=0), axis=0)
    kr_pool_t = jnp.swapaxes(cache_mla_krope, 2, 3)
    fk_pool_t = jnp.transpose(cache_fox_k, (0, 1, 3, 4, 2)).reshape(cache_fox_k.shape[0], -1, FOX_KV_HEADS * FOX_HD, PAGE_SIZE)
    fv_pool_t = jnp.transpose(cache_fox_v, (0, 1, 3, 4, 2)).reshape(cache_fox_v.shape[0], -1, FOX_KV_HEADS * FOX_HD, PAGE_SIZE)
    lf_pool_t = jnp.swapaxes(cache_fox_logf, 2, 3)
    mem_flat = mem_prompt.reshape(bp * m_tok, d)

    outs = {k: [] for k in ("lat_p", "kr_p", "hg_p", "fk_p", "fv_p", "flf_p", "mk_p", "mv_p",
                            "lat_s", "kr_s", "hg_s", "fk_s", "fv_s", "flf_s")}

    def sample_rows(a, width):
        return _pad_axis(a.reshape(bs, ts, width), 1, 8)

    for l in range(depth):
        ng = norm_gains[l]
        x = _ffn_half(x, ng[N_FFN1_PRE], ng[N_FFN1_POST], w_ffn_gate[l, 0], w_ffn_up[l, 0], w_ffn_down[l, 0])
        if l % 2 == 0:
            e = l // 2
            q, kp, vp, c, krp, hq, logf, kk, hi, gate = _even_inproj(
                x, ng[N_MIX_PRE], cos_t, sin_t, w_in_ab[e], g_mla_q[e], g_mla_kv[e], w_mla_uq[e], w_mla_uk[e],
                w_mla_uv[e], lb_all[e], n_p)
            kr = krp[:, MLA_NOPE:MLA_NOPE + MLA_ROPE]
            oa_p = _flash_causal(q, kp, vp, bp, tp, hps=2, shared_k=False)
            ob_p, s_p = _hgrn(hq, kk, hi, logf, gate, jnp.zeros((bp, HG_HEADS, HG_DK, HG_DV), F32), g_hgrn_out[e],
                              bp, tp, _pick_tile(tp, HGRN_BLOCK), HGRN_CHUNK, BF16)
            w_o = w_out_ab[e]
            x = _outproj_residual(x, ng[N_MIX_POST], [oa_p, ob_p], [w_o[:MLA_HEADS * MLA_V], w_o[MLA_HEADS * MLA_V:]], 0)
            w_uk_t = jnp.zeros((MLA_HEADS, LANES, MLA_KV_LORA), F32).at[:, :MLA_NOPE].set(jnp.transpose(w_mla_uk[e], (1, 2, 0)))
            qa = _headmm(q, w_uk_t, n_s, n_p, BF16)
            qa = jnp.transpose(qa.reshape(MLA_HEADS, bs, ts, MLA_KV_LORA), (1, 0, 2, 3)).reshape(bs, MLA_HEADS * ts, MLA_KV_LORA)
            qr = q[:, n_p:, MLA_NOPE:MLA_NOPE + MLA_ROPE].reshape(MLA_HEADS, bs, ts, MLA_ROPE)
            qr = jnp.transpose(qr, (1, 0, 2, 3)).reshape(bs, MLA_HEADS * ts, MLA_ROPE)
            c_new = _pad_axis(c[n_p:].reshape(bs, ts, MLA_KV_LORA), 1, PAGE_SIZE)
            kr_new_t = _pad_axis(jnp.swapaxes(kr[n_p:].reshape(bs, ts, MLA_ROPE), 1, 2), 2, PAGE_SIZE)
            att = _mla_paged(page_table, qa, qr, c_new, kr_new_t, cache_mla_latent, kr_pool_t, e, ts)
            att = jnp.transpose(att.reshape(bs, MLA_HEADS, ts, MLA_KV_LORA), (1, 0, 2, 3)).reshape(MLA_HEADS, n_s, MLA_KV_LORA)
            oa_s = _headmm(att, jnp.transpose(w_mla_uv[e], (1, 0, 2)), n_s, 0, BF16)
            oa_s = jnp.transpose(oa_s, (1, 0, 2)).reshape(n_s, MLA_HEADS * MLA_V)
            pads = [sample_rows(a[n_p:], hg_w).reshape(bs * 8, hg_w) for a in (hq, kk, hi, logf, gate)]
            ob_s, s_s = _hgrn(*pads, state_hgrn[e], g_hgrn_out[e], bs, 8, 8, 8, F32)
            ob_s = ob_s.reshape(bs, 8, hg_w)[:, :ts].reshape(n_s, hg_w)
            x = _outproj_residual(x, ng[N_MIX_POST], [oa_s, ob_s], [w_o[:MLA_HEADS * MLA_V], w_o[MLA_HEADS * MLA_V:]], n_p)
            outs["lat_p"].append(c[:n_p].reshape(bp, tp, MLA_KV_LORA))
            outs["kr_p"].append(kr[:n_p].reshape(bp, tp, MLA_ROPE))
            outs["hg_p"].append(s_p)
            outs["lat_s"].append(c[n_p:].reshape(bs, ts, MLA_KV_LORA))
            outs["kr_s"].append(kr[n_p:].reshape(bs, ts, MLA_ROPE))
            outs["hg_s"].append(s_s)
        else:
            o = l // 2
            qa, ka, v2, kf, vf, logf, cum = _odd_inproj(x, ng[N_MIX_PRE], w_in_c[o], b_fox_f[o], n_p, tp, ts)
            att_p = _flash_causal(qa, ka, v2, bp, tp, hps=FOX_HEADS // FOX_KV_HEADS, shared_k=True)
            x = _outproj_residual(x, ng[N_MIX_POST], [att_p], [w_out_c[o]], 0)
            g = FOX_HEADS // FOX_KV_HEADS
            qs = qa[:, n_p:, :FOX_HD].reshape(FOX_HEADS, bs, ts, FOX_HD)
            onehot = jnp.asarray(np.arange(FOX_HEADS)[:, None] // g == np.arange(FOX_KV_HEADS)[None, :], BF16)
            q_bd = jnp.einsum('hbtd,hn->bhtnd', qs, onehot).reshape(bs, FOX_HEADS * ts, FOX_KV_HEADS * FOX_HD)
            cum_s = cum[n_p:, :FOX_HEADS].reshape(bs, ts, FOX_HEADS)
            cq = jnp.swapaxes(cum_s, 1, 2).reshape(bs, FOX_HEADS * ts, 1)
            neg_cq = jnp.broadcast_to(-jnp.swapaxes(cum_s, 1, 2)[:, :, None, :], (bs, FOX_HEADS, ts, ts))
            neg_cq = _pad_axis(neg_cq.reshape(bs, FOX_HEADS * ts, ts), 2, LANES)
            expand = jnp.asarray(np.repeat(np.eye(FOX_HEADS), ts, axis=0), BF16)
            k_new_t = _pad_axis(jnp.swapaxes(kf[n_p:].reshape(bs, ts, -1), 1, 2), 2, PAGE_SIZE)
            v_new_t = _pad_axis(jnp.swapaxes(vf[n_p:].reshape(bs, ts, -1), 1, 2), 2, PAGE_SIZE)
            att_s = _fox_paged(page_table, q_bd, cq, neg_cq, expand, k_new_t, v_new_t, fk_pool_t, fv_pool_t, lf_pool_t, o, ts)
            att_s = jnp.transpose(att_s.reshape(bs, FOX_HEADS, ts, FOX_HD), (0, 2, 1, 3)).reshape(n_s, FOX_HEADS * FOX_HD)
            x = _outproj_residual(x, ng[N_MIX_POST], [att_s], [w_out_c[o]], n_p)
            lf = logf[:, :FOX_HEADS]
            outs["fk_p"].append(kf[:n_p].reshape(bp, tp, FOX_KV_HEADS, FOX_HD))
            outs["fv_p"].append(vf[:n_p].reshape(bp, tp, FOX_KV_HEADS, FOX_HD))
            outs["flf_p"].append(lf[:n_p].reshape(bp, tp, FOX_HEADS))
            outs["fk_s"].append(kf[n_p:].reshape(bs, ts, FOX_KV_HEADS, FOX_HD))
            outs["fv_s"].append(vf[n_p:].reshape(bs, ts, FOX_KV_HEADS, FOX_HD))
            outs["flf_s"].append(lf[n_p:].reshape(bs, ts, FOX_HEADS))
        mk, mv = _norm_matmul(mem_flat, ng[N_MEM_SRC], [w_mem_k[l], w_mem_v[l]], [1.0, 1.0], [F32, F32])
        mk, mv = mk.reshape(bp, m_tok, mem_w), mv.reshape(bp, m_tok, mem_w)
        (mq,) = _norm_matmul(x, ng[N_MEM_PRE], [w_mem_q[l]], [float(MEM_HD ** -0.5)], [F32])
        om_p = _mem_attn(mq[:n_p].reshape(bp, tp, mem_w), mk, mv, 1, _pick_tile(tp, TOKEN_TILE), BF16)
        x = _outproj_residual(x, ng[N_MEM_POST], [om_p.reshape(n_p, mem_w)], [w_mem_o[l]], 0)
        grp = _pick_tile(bs, MEM_GROUP) if bs % 8 == 0 else 1
        om_s = _mem_attn(sample_rows(mq[n_p:], mem_w), cache_mem_k[l].reshape(bs, m_tok, mem_w),
                         cache_mem_v[l].reshape(bs, m_tok, mem_w), grp, 8, F32)
        x = _outproj_residual(x, ng[N_MEM_POST], [om_s[:, :ts].reshape(n_s, mem_w)], [w_mem_o[l]], n_p)
        outs["mk_p"].append(mk.reshape(bp, m_tok, MEM_HEADS, MEM_HD))
        outs["mv_p"].append(mv.reshape(bp, m_tok, MEM_HEADS, MEM_HD))
        x = _ffn_half(x, ng[N_FFN2_PRE], ng[N_FFN2_POST], w_ffn_gate[l, 1], w_ffn_up[l, 1], w_ffn_down[l, 1])

    st = lambda k: jnp.stack(outs[k])
    return (x[:n_p].reshape(bp, tp, d), x[n_p:].reshape(bs, ts, d),
            st("lat_p"), st("kr_p"), st("hg_p"), st("fk_p"), st("fv_p"), st("flf_p"), st("mk_p"), st("mv_p"),
            st("lat_s"), st("kr_s"), st("hg_s"), st("fk_s"), st("fv_s"), st("flf_s"))
```

```python
import functools
import math

import jax
import jax.numpy as jnp
import numpy as np
from jax import lax
from jax.experimental import pallas as pl
from jax.experimental.pallas import tpu as pltpu

F32 = jnp.float32
BF16 = jnp.bfloat16

MLA_HEADS = 8
MLA_NOPE = 64
MLA_ROPE = 32
MLA_V = 64
MLA_KV_LORA = 256
MLA_Q_LORA = 384
ROPE_THETA = 10000.0
HG_HEADS = 4
HG_DK = 128
HG_DV = 128
FOX_HEADS = 16
FOX_KV_HEADS = 4
FOX_HD = 64
MEM_HEADS = 4
MEM_HD = 128
PAGE_SIZE = 128
EPS = 1e-6
N_FFN1_PRE, N_FFN1_POST, N_MIX_PRE, N_MIX_POST, N_MEM_PRE, N_MEM_POST, N_FFN2_PRE, N_FFN2_POST, N_MEM_SRC = range(9)

LANES = 128
VMEM_LIMIT = 56 * 1024 * 1024
TOKEN_TILE = 512
INPROJ_TILE = 256
FFN_TILE = 768
FFN_CHUNK = 512
ATTN_BLOCK = 512
HGRN_BLOCK = 256
HGRN_CHUNK = 16
PAGES_PER_STEP = 32
MEM_GROUP = 8


def _params(*sem):
    return pltpu.CompilerParams(dimension_semantics=sem, vmem_limit_bytes=VMEM_LIMIT)


def _rms(x, g):
    return x * lax.rsqrt(jnp.mean(x * x, axis=-1, keepdims=True) + EPS) * g


def _dot(a, b):
    return jnp.dot(a.astype(BF16), b.astype(BF16), preferred_element_type=F32)


def _dot_nt(a, b):
    return lax.dot_general(a.astype(BF16), b.astype(BF16), (((1,), (1,)), ((), ())), preferred_element_type=F32)


def _dot_tn(a, b):
    return lax.dot_general(a.astype(BF16), b.astype(BF16), (((0,), (0,)), ((), ())), preferred_element_type=F32)


def _split3(a):
    hi = a.astype(BF16)
    r = a - hi.astype(F32)
    mid = r.astype(BF16)
    lo = (r - mid.astype(F32)).astype(BF16)
    return hi, mid, lo


def _dot_exact_rhs(m01, a):
    hi, mid, lo = _split3(a)
    return (jnp.dot(m01, hi, preferred_element_type=F32) + jnp.dot(m01, mid, preferred_element_type=F32)
            + jnp.dot(m01, lo, preferred_element_type=F32))


def _dot_exact_lhs(a, m01):
    hi, mid, lo = _split3(a)
    return (jnp.dot(hi, m01, preferred_element_type=F32) + jnp.dot(mid, m01, preferred_element_type=F32)
            + jnp.dot(lo, m01, preferred_element_type=F32))


def _log_sigmoid(x):
    return jnp.minimum(x, 0.0) - jnp.log(1.0 + jnp.exp(-jnp.abs(x)))


def _pick_tile(n, target):
    t = min(n, target)
    while n % t:
        t -= 8
    assert t > 0 and t % 8 == 0, (n, target)
    return t


def _full(shape):
    return pl.BlockSpec(shape, lambda *_: (0,) * len(shape))


def _resident(shape):
    return pl.BlockSpec(shape, lambda *_: (0,) * len(shape), pipeline_mode=pl.Buffered(1))


def _ffn_kernel(x_ref, gpre_ref, gpost_ref, wg_ref, wu_ref, wd_ref, o_ref, acc_ref, *, chunks):
    x = x_ref[...]
    h = _rms(x, gpre_ref[...]).astype(BF16)
    for idx, (c0, cw) in enumerate(chunks):
        g = jnp.dot(h, wg_ref[:, c0:c0 + cw], preferred_element_type=F32)
        u = jnp.dot(h, wu_ref[:, c0:c0 + cw], preferred_element_type=F32)
        a = (g * jax.nn.sigmoid(g) * u).astype(BF16)
        d = jnp.dot(a, wd_ref[c0:c0 + cw, :], preferred_element_type=F32)
        if idx == 0:
            acc_ref[...] = d
        else:
            acc_ref[...] += d
    o_ref[...] = x + 0.5 * _rms(acc_ref[...], gpost_ref[...])


def _ffn_half(x, g_pre, g_post, w_gate, w_up, w_down):
    n, d = x.shape
    f = w_gate.shape[1]
    tm = _pick_tile(n, FFN_TILE)
    chunks = tuple((c0, min(FFN_CHUNK, f - c0)) for c0 in range(0, f, FFN_CHUNK))
    return pl.pallas_call(
        functools.partial(_ffn_kernel, chunks=chunks),
        grid=(n // tm,),
        in_specs=[pl.BlockSpec((tm, d), lambda i: (i, 0)), _full((1, d)), _full((1, d)),
                  _resident((d, f)), _resident((d, f)), _resident((f, d))],
        out_specs=pl.BlockSpec((tm, d), lambda i: (i, 0)),
        out_shape=jax.ShapeDtypeStruct((n, d), F32),
        scratch_shapes=[pltpu.VMEM((tm, d), F32)],
        compiler_params=_params("parallel"),
        name="ffn_half",
    )(x, g_pre.reshape(1, d), g_post.reshape(1, d), w_gate.astype(BF16), w_up.astype(BF16), w_down.astype(BF16))


def _norm_matmul_kernel(x_ref, g_ref, *refs, scales):
    n_out = len(scales)
    h = _rms(x_ref[...], g_ref[...]).astype(BF16)
    for w_ref, o_ref, s in zip(refs[:n_out], refs[n_out:], scales):
        y = jnp.dot(h, w_ref[...], preferred_element_type=F32)
        o_ref[...] = (y if s == 1.0 else y * s).astype(o_ref.dtype)


def _norm_matmul(x, g, ws, scales, dtypes):
    n, d = x.shape
    tm = _pick_tile(n, TOKEN_TILE)
    return pl.pallas_call(
        functools.partial(_norm_matmul_kernel, scales=tuple(scales)),
        grid=(n // tm,),
        in_specs=[pl.BlockSpec((tm, d), lambda i: (i, 0)), _full((1, d))] + [_resident(w.shape) for w in ws],
        out_specs=[pl.BlockSpec((tm, w.shape[1]), lambda i: (i, 0)) for w in ws],
        out_shape=[jax.ShapeDtypeStruct((n, w.shape[1]), dt) for w, dt in zip(ws, dtypes)],
        compiler_params=_params("parallel"),
        name="norm_matmul",
    )(x, g.reshape(1, d), *[w.astype(BF16) for w in ws])


def _outproj_kernel(x_ref, g_ref, *refs, n_parts):
    acc = None
    for a_ref, w_ref in zip(refs[:n_parts], refs[n_parts:2 * n_parts]):
        y = jnp.dot(a_ref[...].astype(BF16), w_ref[...], preferred_element_type=F32)
        acc = y if acc is None else acc + y
    refs[2 * n_parts][...] = x_ref[...] + _rms(acc, g_ref[...])


def _outproj_residual(x, g, parts, ws, row0):
    n, d = x.shape
    rows = parts[0].shape[0]
    tm = _pick_tile(rows, TOKEN_TILE)
    assert row0 % tm == 0
    off = row0 // tm
    return pl.pallas_call(
        functools.partial(_outproj_kernel, n_parts=len(parts)),
        grid=(rows // tm,),
        in_specs=[pl.BlockSpec((tm, d), lambda i: (i + off, 0)), _full((1, d))]
        + [pl.BlockSpec((tm, p.shape[1]), lambda i: (i, 0)) for p in parts] + [_resident(w.shape) for w in ws],
        out_specs=pl.BlockSpec((tm, d), lambda i: (i + off, 0)),
        out_shape=jax.ShapeDtypeStruct((n, d), F32),
        input_output_aliases={0: 0},
        compiler_params=_params("parallel"),
        name="outproj_residual",
    )(x, g.reshape(1, d), *parts, *[w.astype(BF16) for w in ws])


def _even_inproj_kernel(x_ref, gpre_ref, cos_ref, sin_ref, cost_ref, sint_ref, wq_ref, wc_ref, wkr_ref, wh_ref, wuqt_ref,
                        wuk_ref, wuvt_ref, gq_ref, gkv_ref, lb_ref,
                        qt_ref, kp_ref, vt_ref, c_ref, krp_ref, hq_ref, logf_ref, kk_ref, hi_ref, gate_ref, *, q_scale):
    h = _rms(x_ref[...], gpre_ref[...]).astype(BF16)
    cos_t = cos_ref[...]
    sin_t = sin_ref[...]
    cos_tt = cost_ref[...]
    sin_tt = sint_ref[...]
    nh = kp_ref.shape[0]
    qn = _rms(jnp.dot(h, wq_ref[...], preferred_element_type=F32), gq_ref[...]).astype(BF16)
    q2t = _dot_nt(wuqt_ref[...], qn)
    for i in range(nh):
        a = q2t[i * LANES:(i + 1) * LANES, :]
        b = q2t[(nh + i) * LANES:(nh + i + 1) * LANES, :]
        qt_ref[i * LANES:(i + 1) * LANES, :] = ((a * cos_tt + b * sin_tt) * q_scale).astype(qt_ref.dtype)
    c = _rms(jnp.dot(h, wc_ref[...], preferred_element_type=F32), gkv_ref[...])
    c_ref[...] = c
    kr2 = jnp.dot(h, wkr_ref[...], preferred_element_type=F32)
    krp = kr2[:, :LANES] * cos_t + kr2[:, LANES:] * sin_t
    krp_ref[...] = krp
    cb = c.astype(BF16)
    kn = jnp.dot(cb, wuk_ref[...], preferred_element_type=F32)
    for i in range(nh):
        kp_ref[i] = (kn[:, i * LANES:(i + 1) * LANES] + krp).astype(kp_ref.dtype)
    vt_ref[...] = _dot_nt(wuvt_ref[...], cb).astype(vt_ref.dtype)
    hh = jnp.dot(h, wh_ref[...], preferred_element_type=F32)
    w = hq_ref.shape[1]
    lb = lb_ref[...]
    z = hh[:, w:2 * w]
    hg = hh[:, 3 * w:]
    hq_ref[...] = hh[:, :w]
    logf_ref[...] = jnp.log(lb + (1.0 - lb) * jax.nn.sigmoid(z))
    kk_ref[...] = (1.0 - lb) * jax.nn.sigmoid(-z)
    hi_ref[...] = hh[:, 2 * w:3 * w]
    gate_ref[...] = hg * jax.nn.sigmoid(hg)


def _rot_cols(w):
    half = w.shape[-1] // 2
    return jnp.concatenate([-w[..., half:], w[..., :half]], axis=-1)


def _even_inproj(x, g_pre, cos_t, sin_t, cos_tt, sin_tt, w_in, g_q, g_kv, w_uq, w_uk, w_uv, lb, n_prompt):
    n, d = x.shape
    tm = _pick_tile(math.gcd(n_prompt, n - n_prompt), INPROJ_TILE)
    nh, hw = MLA_HEADS, HG_HEADS * HG_DK
    o_q, o_c, o_kr = MLA_Q_LORA, MLA_Q_LORA + MLA_KV_LORA, MLA_Q_LORA + MLA_KV_LORA + MLA_ROPE
    wq, wc, wkr, wh = w_in[:, :o_q], w_in[:, o_q:o_c], w_in[:, o_c:o_kr], w_in[:, o_kr:]
    wkr2 = jnp.zeros((d, 2 * LANES), F32)
    wkr2 = wkr2.at[:, MLA_NOPE:MLA_NOPE + MLA_ROPE].set(wkr).at[:, LANES + MLA_NOPE:LANES + MLA_NOPE + MLA_ROPE].set(_rot_cols(wkr))
    uq = w_uq.reshape(MLA_Q_LORA, nh, MLA_NOPE + MLA_ROPE)
    wuq2 = jnp.zeros((MLA_Q_LORA, 2, nh, LANES), F32)
    wuq2 = wuq2.at[:, 0, :, :MLA_NOPE + MLA_ROPE].set(uq)
    wuq2 = wuq2.at[:, 1, :, MLA_NOPE:MLA_NOPE + MLA_ROPE].set(_rot_cols(uq[..., MLA_NOPE:]))
    wuq2 = wuq2.reshape(MLA_Q_LORA, 2 * nh * LANES)
    wuk = jnp.zeros((MLA_KV_LORA, nh, LANES), F32).at[:, :, :MLA_NOPE].set(w_uk).reshape(MLA_KV_LORA, nh * LANES)
    wuv = jnp.zeros((MLA_KV_LORA, nh // 2, 2, 2, MLA_V), F32)
    uv = w_uv.reshape(MLA_KV_LORA, nh // 2, 2, MLA_V)
    wuv = wuv.at[:, :, 0, 0].set(uv[:, :, 0]).at[:, :, 1, 1].set(uv[:, :, 1]).reshape(MLA_KV_LORA, nh * LANES)
    ws = [w.astype(BF16) for w in (wq, wc, wkr2, wh, wuq2.T, wuk, wuv.T)]
    tok = lambda w: pl.BlockSpec((tm, w), lambda i: (i, 0))
    tok_t = lambda r: pl.BlockSpec((r, tm), lambda i: (0, i))
    head = pl.BlockSpec((nh, tm, LANES), lambda i: (0, i, 0))
    outs = pl.pallas_call(
        functools.partial(_even_inproj_kernel, q_scale=float((MLA_NOPE + MLA_ROPE) ** -0.5)),
        grid=(n // tm,),
        in_specs=[tok(d), _full((1, d)), tok(LANES), tok(LANES), tok_t(LANES), tok_t(LANES)]
        + [_resident(w.shape) for w in ws] + [_full((1, MLA_Q_LORA)), _full((1, MLA_KV_LORA)), _full((1, hw))],
        out_specs=[tok_t(nh * LANES), head, tok_t(nh * LANES), tok(MLA_KV_LORA), tok(LANES)] + [tok(hw)] * 5,
        out_shape=[jax.ShapeDtypeStruct((nh * LANES, n), BF16), jax.ShapeDtypeStruct((nh, n, LANES), BF16),
                   jax.ShapeDtypeStruct((nh * LANES, n), BF16),
                   jax.ShapeDtypeStruct((n, MLA_KV_LORA), F32), jax.ShapeDtypeStruct((n, LANES), F32)]
        + [jax.ShapeDtypeStruct((n, hw), F32)] * 5,
        compiler_params=_params("parallel"),
        name="even_inproj",
    )(x, g_pre.reshape(1, d), cos_t, sin_t, cos_tt, sin_tt, *ws, g_q.reshape(1, -1), g_kv.reshape(1, -1),
      lb.reshape(1, -1))
    return outs


def _odd_inproj_kernel(x_ref, gpre_ref, wqt_ref, wk_ref, wv2t_ref, wkf_ref, wvf_ref, wf_ref, bf_ref,
                       selqt_ref, selk_ref,
                       qat_ref, ka_ref, v2t_ref, kf_ref, vf_ref, logf_ref, cum_ref, carry_ref,
                       *, n_prompt_tiles, tiles_per_seq, sample_shift):
    i = pl.program_id(0)
    tm = x_ref.shape[0]
    is_sample = i >= n_prompt_tiles

    @pl.when(i == 0)
    def _():
        carry_ref[...] = jnp.zeros(carry_ref.shape, F32)

    h = _rms(x_ref[...], gpre_ref[...]).astype(BF16)
    lane = lax.broadcasted_iota(jnp.int32, (tm, LANES), 1)
    fz = jnp.dot(h, wf_ref[...], preferred_element_type=F32) + bf_ref[...]
    logf = jnp.where(lane < FOX_HEADS, _log_sigmoid(fz), 0.0)
    logf_ref[...] = logf
    row = lax.broadcasted_iota(jnp.int32, (tm, tm), 0)
    col = lax.broadcasted_iota(jnp.int32, (tm, tm), 1)
    shift = jnp.where(is_sample, sample_shift, 30)
    tri = jnp.where((col <= row) & ((row >> shift) == (col >> shift)), 1.0, 0.0).astype(BF16)
    fresh = jnp.logical_or(is_sample, i % tiles_per_seq == 0)
    carry = jnp.where(fresh, 0.0, carry_ref[...])
    cum = _dot_exact_rhs(tri, logf) + carry
    cum_ref[...] = cum
    carry_ref[...] = cum[tm - 1:tm, :]
    hi, mid, lo = _split3(cum)
    hi = jnp.where(lane == FOX_HEADS, 1.0, hi)
    qat_ref[...] = (_dot_nt(wqt_ref[...], h) + _dot_nt(selqt_ref[0], hi) + _dot_nt(selqt_ref[1], mid)
                    + _dot_nt(selqt_ref[2], lo)).astype(qat_ref.dtype)
    ka = (jnp.dot(h, wk_ref[...], preferred_element_type=F32) + jnp.dot(hi, selk_ref[0], preferred_element_type=F32)
          + jnp.dot(mid, selk_ref[1], preferred_element_type=F32) + jnp.dot(lo, selk_ref[2], preferred_element_type=F32))
    for j in range(ka_ref.shape[0]):
        ka_ref[j] = ka[:, j * LANES:(j + 1) * LANES].astype(ka_ref.dtype)
    v2t_ref[...] = _dot_nt(wv2t_ref[...], h).astype(v2t_ref.dtype)
    kf_ref[...] = jnp.dot(h, wkf_ref[...], preferred_element_type=F32)
    vf_ref[...] = jnp.dot(h, wvf_ref[...], preferred_element_type=F32)


def _fox_selectors():
    nq, nk, g = FOX_HEADS, FOX_KV_HEADS, FOX_HEADS // FOX_KV_HEADS
    selq = np.zeros((3, LANES, nq * LANES), np.float32)
    selk = np.zeros((3, LANES, nk * LANES), np.float32)
    base = FOX_HD
    ones_row = FOX_HEADS
    for hd in range(nq):
        n, gi = divmod(hd, g)
        for t in range(3):
            selq[t, hd, hd * LANES + base + t] = 1.0
            selq[0, ones_row, hd * LANES + base + 3 + 3 * gi + t] = 1.0
            selk[t, hd, n * LANES + base + 3 + 3 * gi + t] = -1.0
    for n in range(nk):
        selk[0, ones_row, n * LANES + base:n * LANES + base + 3] = 1.0
    return jnp.asarray(np.swapaxes(selq, 1, 2), BF16), jnp.asarray(selk, BF16)


def _odd_inproj(x, g_pre, w_in, b_f, n_prompt, seq_prompt, seq_sample):
    n, d = x.shape
    tm = _pick_tile(math.gcd(math.gcd(n_prompt, n - n_prompt), seq_prompt), INPROJ_TILE)
    assert n_prompt % tm == 0 and seq_prompt % tm == 0 and tm % seq_sample == 0
    assert seq_sample & (seq_sample - 1) == 0, "sample sequences must be a power of two long"
    nq, nk, hd = FOX_HEADS, FOX_KV_HEADS, FOX_HD
    o_q, o_k, o_v = nq * hd, nq * hd + nk * hd, nq * hd + 2 * nk * hd
    w_q, w_k, w_v, w_f = w_in[:, :o_q], w_in[:, o_q:o_k], w_in[:, o_k:o_v], w_in[:, o_v:]
    wq = jnp.zeros((d, nq, LANES), F32).at[:, :, :hd].set(w_q.reshape(d, nq, hd) * float(hd ** -0.5)).reshape(d, nq * LANES)
    wk = jnp.zeros((d, nk, LANES), F32).at[:, :, :hd].set(w_k.reshape(d, nk, hd)).reshape(d, nk * LANES)
    wv2 = jnp.zeros((d, nk, 2, 2, hd), F32)
    wv2 = wv2.at[:, :, 0, 0].set(w_v.reshape(d, nk, hd)).at[:, :, 1, 1].set(w_v.reshape(d, nk, hd)).reshape(d, 2 * nk * LANES)
    wf = jnp.zeros((d, LANES), F32).at[:, :nq].set(w_f)
    bf = jnp.zeros((1, LANES), F32).at[0, :nq].set(b_f)
    selq_t, selk = _fox_selectors()
    ws = [w.astype(BF16) for w in (wq.T, wk, wv2.T, w_k, w_v, wf)]
    tok = lambda w: pl.BlockSpec((tm, w), lambda i: (i, 0))
    tok_t = lambda r: pl.BlockSpec((r, tm), lambda i: (0, i))
    head = lambda k: pl.BlockSpec((k, tm, LANES), lambda i: (0, i, 0))
    return pl.pallas_call(
        functools.partial(_odd_inproj_kernel, n_prompt_tiles=n_prompt // tm, tiles_per_seq=seq_prompt // tm,
                          sample_shift=int(seq_sample).bit_length() - 1),
        grid=(n // tm,),
        in_specs=[tok(d), _full((1, d))] + [_resident(w.shape) for w in ws]
        + [_full((1, LANES)), _resident(selq_t.shape), _resident(selk.shape)],
        out_specs=[tok_t(nq * LANES), head(nk), tok_t(2 * nk * LANES), tok(nk * hd), tok(nk * hd), tok(LANES), tok(LANES)],
        out_shape=[jax.ShapeDtypeStruct((nq * LANES, n), BF16), jax.ShapeDtypeStruct((nk, n, LANES), BF16),
                   jax.ShapeDtypeStruct((2 * nk * LANES, n), BF16), jax.ShapeDtypeStruct((n, nk * hd), F32),
                   jax.ShapeDtypeStruct((n, nk * hd), F32), jax.ShapeDtypeStruct((n, LANES), F32),
                   jax.ShapeDtypeStruct((n, LANES), F32)],
        scratch_shapes=[pltpu.VMEM((1, LANES), F32)],
        compiler_params=_params("arbitrary"),
        name="odd_inproj",
    )(x, g_pre.reshape(1, d), *ws, bf, selq_t, selk)


def _flash_kernel(qi_ref, kj_ref, qt_ref, k_ref, vt_ref, o_ref, m_ref, l_ref, acc_ref, *, hps, shared_k):
    p = pl.program_id(2)
    i = qi_ref[p]
    j = kj_ref[p]
    tk, tq = k_ref.shape[1], qt_ref.shape[1]

    @pl.when(j == 0)
    def _():
        m_ref[...] = jnp.full(m_ref.shape, -jnp.inf, F32)
        l_ref[...] = jnp.zeros(l_ref.shape, F32)
        acc_ref[...] = jnp.zeros(acc_ref.shape, F32)

    def update(masked):
        if masked:
            keep = lax.broadcasted_iota(jnp.int32, (tk, tq), 0) <= lax.broadcasted_iota(jnp.int32, (tk, tq), 1)
        for hh in range(hps):
            s = jnp.dot(k_ref[0 if shared_k else hh], qt_ref[hh * LANES:(hh + 1) * LANES, :],
                        preferred_element_type=F32)
            if masked:
                s = jnp.where(keep, s, -jnp.inf)
            m_prev = m_ref[hh]
            m_new = jnp.maximum(m_prev, jnp.max(s, axis=0, keepdims=True))
            alpha = jnp.exp(m_prev - m_new)
            pe = jnp.exp(s - m_new)
            l_ref[hh] = alpha * l_ref[hh] + jnp.sum(pe, axis=0, keepdims=True)
            vt = vt_ref[(hh % 2) * LANES:(hh % 2 + 1) * LANES, :]
            acc_ref[hh] = alpha * acc_ref[hh] + jnp.dot(vt, pe.astype(BF16), preferred_element_type=F32)
            m_ref[hh] = m_new

    @pl.when(j < i)
    def _():
        update(False)

    @pl.when(j == i)
    def _():
        update(True)
        for r in range(hps // 2):
            ot = acc_ref[2 * r] / l_ref[2 * r] + acc_ref[2 * r + 1] / l_ref[2 * r + 1]
            o_ref[:, r * LANES:(r + 1) * LANES] = ot.T.astype(o_ref.dtype)


def _flash_causal(qt, k, vt, n_seq, seq_len, hps, shared_k):
    nh = qt.shape[0] // LANES
    groups = nh // hps
    blk = _pick_tile(seq_len, ATTN_BLOCK)
    nb = seq_len // blk
    pairs = [(a, b) for a in range(nb) for b in range(a + 1)]
    qi = jnp.asarray([a for a, _ in pairs], jnp.int32)
    kj = jnp.asarray([b for _, b in pairs], jnp.int32)
    kh = 1 if shared_k else hps
    gs = pltpu.PrefetchScalarGridSpec(
        num_scalar_prefetch=2, grid=(n_seq, groups, len(pairs)),
        in_specs=[pl.BlockSpec((hps * LANES, blk), lambda b, g, p, qi, kj: (g, b * nb + qi[p])),
                  pl.BlockSpec((kh, blk, LANES), lambda b, g, p, qi, kj: (g, b * nb + kj[p], 0)),
                  pl.BlockSpec((2 * LANES, blk), lambda b, g, p, qi, kj: (g, b * nb + kj[p]))],
        out_specs=pl.BlockSpec((blk, hps * LANES // 2), lambda b, g, p, qi, kj: (b * nb + qi[p], g)),
        scratch_shapes=[pltpu.VMEM((hps, 1, blk), F32), pltpu.VMEM((hps, 1, blk), F32), pltpu.VMEM((hps, LANES, blk), F32)])
    return pl.pallas_call(
        functools.partial(_flash_kernel, hps=hps, shared_k=shared_k),
        grid_spec=gs,
        out_shape=jax.ShapeDtypeStruct((n_seq * seq_len, nh * LANES // 2), BF16),
        compiler_params=_params("parallel", "parallel", "arbitrary"),
        name="flash_causal",
    )(qi, kj, qt, k, vt)


def _hgrn_kernel(hq_ref, kk_ref, hi_ref, logf_ref, gate_ref, s0_ref, gout_ref, o_ref, s_ref, obuf_ref, *, chunk):
    j = pl.program_id(1)
    tb = hq_ref.shape[0]
    nh = s_ref.shape[1]
    dk = s_ref.shape[2]

    @pl.when(j == 0)
    def _():
        s_ref[...] = s0_ref[...]

    row = lax.broadcasted_iota(jnp.int32, (chunk, chunk), 0)
    col = lax.broadcasted_iota(jnp.int32, (chunk, chunk), 1)
    causal = col <= row
    tri = jnp.where(causal, 1.0, 0.0).astype(BF16)

    def step(c, carry):
        r0 = pl.multiple_of(c * chunk, chunk)
        for hh in range(nh):
            sl = (pl.ds(r0, chunk), slice(hh * dk, (hh + 1) * dk))
            q, k, v, g = hq_ref[sl], kk_ref[sl], hi_ref[sl], logf_ref[sl]
            cum = _dot_exact_rhs(tri, g)
            a = jnp.zeros((chunk, chunk), F32)
            for s in range(chunk):
                w = q * k[s:s + 1, :] * jnp.exp(jnp.minimum(cum - cum[s:s + 1, :], 0.0))
                a = jnp.where(col == s, jnp.sum(w, axis=-1, keepdims=True), a)
            a = jnp.where(causal, a, 0.0)
            state = s_ref[0, hh]
            o = _dot(a, v) + _dot(q * jnp.exp(cum), state)
            obuf_ref[sl] = o
            last = cum[chunk - 1:chunk, :]
            decay = jnp.broadcast_to(jnp.exp(last), (8, dk)).T[:, 0:1]
            s_ref[0, hh] = decay * state + _dot_tn(k * jnp.exp(last - cum), v)
        return carry

    lax.fori_loop(0, tb // chunk, step, 0)
    for hh in range(nh):
        sl = (slice(None), slice(hh * dk, (hh + 1) * dk))
        o_ref[sl] = (_rms(obuf_ref[sl], gout_ref[sl]) * gate_ref[sl]).astype(o_ref.dtype)


def _hgrn(hq, kk, hi, logf, gate, s0, g_out, n_seq, seq_len, block, chunk, dtype):
    w = hq.shape[1]
    nb = seq_len // block
    tok = pl.BlockSpec((block, w), lambda s, j: (s * nb + j, 0))
    st = pl.BlockSpec((1,) + s0.shape[1:], lambda s, j: (s, 0, 0, 0))
    return pl.pallas_call(
        functools.partial(_hgrn_kernel, chunk=chunk),
        grid=(n_seq, nb),
        in_specs=[tok] * 5 + [st, _full((1, w))],
        out_specs=[tok, st],
        out_shape=[jax.ShapeDtypeStruct((n_seq * seq_len, w), dtype), jax.ShapeDtypeStruct(s0.shape, F32)],
        scratch_shapes=[pltpu.VMEM((block, w), F32)],
        compiler_params=_params("parallel", "arbitrary"),
        name="hgrn2",
    )(hq, kk, hi, logf, gate, s0, g_out.reshape(1, w))


def _headmm_kernel(x_ref, w_ref, o_ref):
    o_ref[...] = jnp.dot(x_ref[...].astype(BF16), w_ref[...], preferred_element_type=F32).astype(o_ref.dtype)


def _headmm(x, w, rows, row0, dtype):
    nh, _, kdim = x.shape
    assert row0 % rows == 0
    off = row0 // rows
    return pl.pallas_call(
        _headmm_kernel,
        grid=(nh,),
        in_specs=[pl.BlockSpec((None, rows, kdim), lambda h: (h, off, 0)),
                  pl.BlockSpec((None,) + w.shape[1:], lambda h: (h, 0, 0))],
        out_specs=pl.BlockSpec((None, rows, w.shape[2]), lambda h: (h, 0, 0)),
        out_shape=jax.ShapeDtypeStruct((nh, rows, w.shape[2]), dtype),
        compiler_params=_params("parallel"),
        name="headmm",
    )(x, w.astype(BF16))


def _online_update(m_ref, l_ref, acc_ref, s, pv):
    m_prev = m_ref[...]
    m_new = jnp.maximum(m_prev, jnp.max(s, axis=-1, keepdims=True))
    alpha = jnp.exp(m_prev - m_new)
    pe = jnp.exp(s - m_new)
    l_ref[...] = alpha * l_ref[...] + jnp.sum(pe, axis=-1, keepdims=True)
    acc_ref[...] = alpha * acc_ref[...] + pv(pe.astype(BF16))
    m_ref[...] = m_new


def _new_token_mask(rows, t_new):
    r = lax.broadcasted_iota(jnp.int32, (rows, LANES), 0)
    c = lax.broadcasted_iota(jnp.int32, (rows, LANES), 1)
    return (c < t_new) & (c <= (r & (t_new - 1)))


def _mla_paged_kernel(pt_ref, qa_ref, qr_ref, cnew_ref, krnew_ref, *refs, pps, t_new):
    ck_refs, kr_refs = refs[:pps], refs[pps:2 * pps]
    o_ref, m_ref, l_ref, acc_ref = refs[2 * pps:]
    j = pl.program_id(1)
    rows = qa_ref.shape[0]

    @pl.when(j == 0)
    def _():
        m_ref[...] = jnp.full(m_ref.shape, -jnp.inf, F32)
        l_ref[...] = jnp.zeros(l_ref.shape, F32)
        acc_ref[...] = jnp.zeros(acc_ref.shape, F32)

    qa = qa_ref[...]
    qr = qr_ref[...]
    cks = [r[...].astype(BF16) for r in ck_refs]
    s = jnp.concatenate([_dot_nt(qa, ck) + _dot(qr, kr[...]) for ck, kr in zip(cks, kr_refs)], axis=-1)

    def pv(p):
        out = None
        for idx, ck in enumerate(cks):
            y = jnp.dot(p[:, idx * PAGE_SIZE:(idx + 1) * PAGE_SIZE], ck, preferred_element_type=F32)
            out = y if out is None else out + y
        return out

    _online_update(m_ref, l_ref, acc_ref, s, pv)

    @pl.when(j == pl.num_programs(1) - 1)
    def _():
        cn = cnew_ref[...].astype(BF16)
        sn = _dot_nt(qa, cn) + _dot(qr, krnew_ref[...])
        sn = jnp.where(_new_token_mask(rows, t_new), sn, -jnp.inf)
        _online_update(m_ref, l_ref, acc_ref, sn, lambda p: jnp.dot(p, cn, preferred_element_type=F32))
        o_ref[...] = (acc_ref[...] / l_ref[...]).astype(o_ref.dtype)


def _mla_paged(page_table, qa, qr, c_new, kr_new_t, lat_pool, kr_pool_t, layer, t_new):
    bsz, rows, cdim = qa.shape
    n_pages = page_table.shape[1]
    pps = min(PAGES_PER_STEP, n_pages)
    assert n_pages % pps == 0 and t_new & (t_new - 1) == 0
    rdim = qr.shape[2]

    def page(i):
        return lambda b, j, pt: (layer, pt[b, j * pps + i], 0, 0)

    per_b = lambda shape: pl.BlockSpec((None,) + shape, lambda b, j, pt: (b, 0, 0))
    gs = pltpu.PrefetchScalarGridSpec(
        num_scalar_prefetch=1, grid=(bsz, n_pages // pps),
        in_specs=[per_b((rows, cdim)), per_b((rows, rdim)), per_b((PAGE_SIZE, cdim)), per_b((rdim, PAGE_SIZE))]
        + [pl.BlockSpec((None, None, PAGE_SIZE, cdim), page(i)) for i in range(pps)]
        + [pl.BlockSpec((None, None, rdim, PAGE_SIZE), page(i)) for i in range(pps)],
        out_specs=per_b((rows, cdim)),
        scratch_shapes=[pltpu.VMEM((rows, 1), F32), pltpu.VMEM((rows, 1), F32), pltpu.VMEM((rows, cdim), F32)])
    return pl.pallas_call(
        functools.partial(_mla_paged_kernel, pps=pps, t_new=t_new),
        grid_spec=gs,
        out_shape=jax.ShapeDtypeStruct((bsz, rows, cdim), F32),
        compiler_params=_params("parallel", "arbitrary"),
        name="mla_paged",
    )(page_table, qa, qr, c_new, kr_new_t, *([lat_pool] * pps), *([kr_pool_t] * pps))


def _fox_paged_kernel(pt_ref, q_ref, cq_ref, ncq_ref, e_ref, knew_ref, vnew_ref, *refs, pps, t_new, n_kv):
    k_refs, v_refs, lf_refs = refs[:pps], refs[pps:2 * pps], refs[2 * pps:3 * pps]
    o_ref, m_ref, l_ref, acc_ref, suf_ref = refs[3 * pps:]
    j = pl.program_id(1)
    rows = q_ref.shape[0]

    @pl.when(j == 0)
    def _():
        m_ref[...] = jnp.full(m_ref.shape, -jnp.inf, F32)
        l_ref[...] = jnp.zeros(l_ref.shape, F32)
        acc_ref[...] = jnp.zeros(acc_ref.shape, F32)
        suf_ref[...] = jnp.zeros(suf_ref.shape, F32)

    q = q_ref[...]
    cq = cq_ref[...]
    e = e_ref[...]
    r_i = lax.broadcasted_iota(jnp.int32, (PAGE_SIZE, PAGE_SIZE), 0)
    c_i = lax.broadcasted_iota(jnp.int32, (PAGE_SIZE, PAGE_SIZE), 1)
    later = jnp.where(r_i > c_i, 1.0, 0.0).astype(BF16)
    run = suf_ref[...]
    ss = []
    for k_ref, lf_ref in zip(k_refs, lf_refs):
        lf = _dot_exact_rhs(e, lf_ref[...])
        bias = _dot_exact_lhs(lf, later) + (run + cq)
        run = run + jnp.sum(lf, axis=-1, keepdims=True)
        ss.append(_dot(q, k_ref[...]) + bias)
    suf_ref[...] = run
    s = jnp.concatenate(ss, axis=-1)

    def pv(p):
        out = None
        for idx, v_ref in enumerate(v_refs):
            y = _dot_nt(p[:, idx * PAGE_SIZE:(idx + 1) * PAGE_SIZE], v_ref[...])
            out = y if out is None else out + y
        return out

    _online_update(m_ref, l_ref, acc_ref, s, pv)

    @pl.when(j == pl.num_programs(1) - 1)
    def _():
        sn = _dot(q, knew_ref[...]) + cq + ncq_ref[...]
        sn = jnp.where(_new_token_mask(rows, t_new), sn, -jnp.inf)
        vn = vnew_ref[...]
        _online_update(m_ref, l_ref, acc_ref, sn, lambda p: _dot_nt(p, vn))
        rk = rows // n_kv
        hd = acc_ref.shape[1] // n_kv
        for n in range(n_kv):
            o_ref[n * rk:(n + 1) * rk, :] = (acc_ref[n * rk:(n + 1) * rk, n * hd:(n + 1) * hd]
                                             / l_ref[n * rk:(n + 1) * rk, :]).astype(o_ref.dtype)


def _fox_paged(page_table, q_bd, cq, neg_cq, expand, k_new_t, v_new_t, k_pool_t, v_pool_t, lf_pool_t, layer, t_new):
    bsz, rows, kd = q_bd.shape
    n_pages = page_table.shape[1]
    nh = lf_pool_t.shape[2]
    pps = min(PAGES_PER_STEP, n_pages)
    assert n_pages % pps == 0 and t_new & (t_new - 1) == 0

    def page(i):
        return lambda b, j, pt: (layer, pt[b, n_pages - 1 - (j * pps + i)], 0, 0)

    per_b = lambda shape: pl.BlockSpec((None,) + shape, lambda b, j, pt: (b, 0, 0))
    gs = pltpu.PrefetchScalarGridSpec(
        num_scalar_prefetch=1, grid=(bsz, n_pages // pps),
        in_specs=[per_b((rows, kd)), per_b((rows, 1)), per_b((rows, LANES)),
                  pl.BlockSpec((rows, nh), lambda b, j, pt: (0, 0)), per_b((kd, PAGE_SIZE)), per_b((kd, PAGE_SIZE))]
        + [pl.BlockSpec((None, None, kd, PAGE_SIZE), page(i)) for i in range(pps)] * 2
        + [pl.BlockSpec((None, None, nh, PAGE_SIZE), page(i)) for i in range(pps)],
        out_specs=per_b((rows, kd // FOX_KV_HEADS)),
        scratch_shapes=[pltpu.VMEM((rows, 1), F32), pltpu.VMEM((rows, 1), F32), pltpu.VMEM((rows, kd), F32),
                        pltpu.VMEM((rows, 1), F32)])
    return pl.pallas_call(
        functools.partial(_fox_paged_kernel, pps=pps, t_new=t_new, n_kv=FOX_KV_HEADS),
        grid_spec=gs,
        out_shape=jax.ShapeDtypeStruct((bsz, rows, kd // FOX_KV_HEADS), F32),
        compiler_params=_params("parallel", "arbitrary"),
        name="fox_paged",
    )(page_table, q_bd, cq, neg_cq, expand, k_new_t, v_new_t, *([k_pool_t] * pps), *([v_pool_t] * pps),
      *([lf_pool_t] * pps))


def _mem_attn_kernel(q_ref, k_ref, v_ref, o_ref, *, n_heads):
    hd = q_ref.shape[2] // n_heads
    for b in range(q_ref.shape[0]):
        for hh in range(n_heads):
            sl = slice(hh * hd, (hh + 1) * hd)
            s = _dot_nt(q_ref[b, :, sl], k_ref[b, :, sl])
            pe = jnp.exp(s - jnp.max(s, axis=-1, keepdims=True))
            p = (pe / jnp.sum(pe, axis=-1, keepdims=True)).astype(BF16)
            o_ref[b, :, sl] = _dot(p, v_ref[b, :, sl]).astype(o_ref.dtype)


def _mem_attn(q, mk, mv, group, tq, dtype):
    nb, t, w = q.shape
    m = mk.shape[1]
    return pl.pallas_call(
        functools.partial(_mem_attn_kernel, n_heads=MEM_HEADS),
        grid=(nb // group, t // tq),
        in_specs=[pl.BlockSpec((group, tq, w), lambda g, i: (g, i, 0)),
                  pl.BlockSpec((group, m, w), lambda g, i: (g, 0, 0)),
                  pl.BlockSpec((group, m, w), lambda g, i: (g, 0, 0))],
        out_specs=pl.BlockSpec((group, tq, w), lambda g, i: (g, i, 0)),
        out_shape=jax.ShapeDtypeStruct((nb, t, w), dtype),
        compiler_params=_params("parallel", "parallel"),
        name="mem_attn",
    )(q, mk, mv)


def _rope_tables(pos):
    half = MLA_ROPE // 2
    inv_freq = ROPE_THETA ** (-jnp.arange(half, dtype=F32) / half)
    ang = pos.astype(F32)[:, None] * inv_freq[None, :]
    cos, sin = jnp.cos(ang), jnp.sin(ang)
    n = pos.shape[0]
    pad = jnp.zeros((n, LANES - MLA_NOPE - MLA_ROPE), F32)
    cos_t = jnp.concatenate([jnp.ones((n, MLA_NOPE), F32), cos, cos, pad], axis=-1)
    sin_t = jnp.concatenate([jnp.zeros((n, MLA_NOPE), F32), sin, sin, pad], axis=-1)
    return cos_t, sin_t


def _pad_axis(a, axis, size):
    pad = [(0, 0)] * a.ndim
    pad[axis] = (0, size - a.shape[axis])
    return jnp.pad(a, pad)


def kernel(x_prompt, x_sample, mem_prompt, cache_mla_latent, cache_mla_krope, state_hgrn, cache_fox_k, cache_fox_v, cache_fox_logf, cache_mem_k, cache_mem_v, page_table, norm_gains, w_ffn_gate, w_ffn_up, w_ffn_down, w_mem_q, w_mem_k, w_mem_v, w_mem_o, w_in_ab, g_mla_q, g_mla_kv, w_mla_uq, w_mla_uk, w_mla_uv, hgrn_lb_logits, g_hgrn_out, w_out_ab, w_in_c, b_fox_f, w_out_c):
    bp, tp, d = x_prompt.shape
    bs, ts, _ = x_sample.shape
    n_p, n_s = bp * tp, bs * ts
    depth = norm_gains.shape[0]
    n_pages = page_table.shape[1]
    m_tok = mem_prompt.shape[1]
    mem_w = MEM_HEADS * MEM_HD
    hg_w = HG_HEADS * HG_DK
    assert ts <= 8 and ts & (ts - 1) == 0

    x = jnp.concatenate([x_prompt.reshape(n_p, d), x_sample.reshape(n_s, d)], axis=0)
    pos = jnp.concatenate([jnp.tile(jnp.arange(tp), bp), n_pages * PAGE_SIZE + jnp.tile(jnp.arange(ts), bs)])
    cos_t, sin_t = _rope_tables(pos)
    cos_tt, sin_tt = cos_t.T, sin_t.T
    lb_all = jnp.cumsum(jax.nn.softmax(hgrn_lb_logits.astype(F32), axis=0), axis=0)
    kr_pool_t = jnp.swapaxes(cache_mla_krope, 2, 3)
    fk_pool_t = jnp.transpose(cache_fox_k, (0, 1, 3, 4, 2)).reshape(cache_fox_k.shape[0], -1, FOX_KV_HEADS * FOX_HD, PAGE_SIZE)
    fv_pool_t = jnp.transpose(cache_fox_v, (0, 1, 3, 4, 2)).reshape(cache_fox_v.shape[0], -1, FOX_KV_HEADS * FOX_HD, PAGE_SIZE)
    lf_pool_t = jnp.swapaxes(cache_fox_logf, 2, 3)
    mem_flat = mem_prompt.reshape(bp * m_tok, d)

    outs = {k: [] for k in ("lat_p", "kr_p", "hg_p", "fk_p", "fv_p", "flf_p", "mk_p", "mv_p",
                            "lat_s", "kr_s", "hg_s", "fk_s", "fv_s", "flf_s")}

    def sample_rows(a, width):
        return _pad_axis(a.reshape(bs, ts, width), 1, 8)

    for l in range(depth):
        ng = norm_gains[l]
        x = _ffn_half(x, ng[N_FFN1_PRE], ng[N_FFN1_POST], w_ffn_gate[l, 0], w_ffn_up[l, 0], w_ffn_down[l, 0])
        if l % 2 == 0:
            e = l // 2
            qt, kp, vt, c, krp, hq, logf, kk, hi, gate = _even_inproj(
                x, ng[N_MIX_PRE], cos_t, sin_t, cos_tt, sin_tt, w_in_ab[e], g_mla_q[e], g_mla_kv[e], w_mla_uq[e],
                w_mla_uk[e], w_mla_uv[e], lb_all[e], n_p)
            kr = krp[:, MLA_NOPE:MLA_NOPE + MLA_ROPE]
            oa_p = _flash_causal(qt, kp, vt, bp, tp, hps=2, shared_k=False)
            ob_p, s_p = _hgrn(hq, kk, hi, logf, gate, jnp.zeros((bp, HG_HEADS, HG_DK, HG_DV), F32), g_hgrn_out[e],
                              bp, tp, _pick_tile(tp, HGRN_BLOCK), HGRN_CHUNK, BF16)
            w_o = w_out_ab[e]
            x = _outproj_residual(x, ng[N_MIX_POST], [oa_p, ob_p], [w_o[:MLA_HEADS * MLA_V], w_o[MLA_HEADS * MLA_V:]], 0)
            w_uk_t = jnp.zeros((MLA_HEADS, LANES, MLA_KV_LORA), F32).at[:, :MLA_NOPE].set(jnp.transpose(w_mla_uk[e], (1, 2, 0)))
            q_s = jnp.swapaxes(qt[:, n_p:].reshape(MLA_HEADS, LANES, n_s), 1, 2)
            qa = _headmm(q_s, w_uk_t, n_s, 0, BF16)
            qa = jnp.transpose(qa.reshape(MLA_HEADS, bs, ts, MLA_KV_LORA), (1, 0, 2, 3)).reshape(bs, MLA_HEADS * ts, MLA_KV_LORA)
            qr = q_s[:, :, MLA_NOPE:MLA_NOPE + MLA_ROPE].reshape(MLA_HEADS, bs, ts, MLA_ROPE)
            qr = jnp.transpose(qr, (1, 0, 2, 3)).reshape(bs, MLA_HEADS * ts, MLA_ROPE)
            c_new = _pad_axis(c[n_p:].reshape(bs, ts, MLA_KV_LORA), 1, PAGE_SIZE)
            kr_new_t = _pad_axis(jnp.swapaxes(kr[n_p:].reshape(bs, ts, MLA_ROPE), 1, 2), 2, PAGE_SIZE)
            att = _mla_paged(page_table, qa, qr, c_new, kr_new_t, cache_mla_latent, kr_pool_t, e, ts)
            att = jnp.transpose(att.reshape(bs, MLA_HEADS, ts, MLA_KV_LORA), (1, 0, 2, 3)).reshape(MLA_HEADS, n_s, MLA_KV_LORA)
            oa_s = _headmm(att, jnp.transpose(w_mla_uv[e], (1, 0, 2)), n_s, 0, BF16)
            oa_s = jnp.transpose(oa_s, (1, 0, 2)).reshape(n_s, MLA_HEADS * MLA_V)
            pads = [sample_rows(a[n_p:], hg_w).reshape(bs * 8, hg_w) for a in (hq, kk, hi, logf, gate)]
            ob_s, s_s = _hgrn(*pads, state_hgrn[e], g_hgrn_out[e], bs, 8, 8, 8, F32)
            ob_s = ob_s.reshape(bs, 8, hg_w)[:, :ts].reshape(n_s, hg_w)
            x = _outproj_residual(x, ng[N_MIX_POST], [oa_s, ob_s], [w_o[:MLA_HEADS * MLA_V], w_o[MLA_HEADS * MLA_V:]], n_p)
            outs["lat_p"].append(c[:n_p].reshape(bp, tp, MLA_KV_LORA))
            outs["kr_p"].append(kr[:n_p].reshape(bp, tp, MLA_ROPE))
            outs["hg_p"].append(s_p)
            outs["lat_s"].append(c[n_p:].reshape(bs, ts, MLA_KV_LORA))
            outs["kr_s"].append(kr[n_p:].reshape(bs, ts, MLA_ROPE))
            outs["hg_s"].append(s_s)
        else:
            o = l // 2
            qat, ka, v2t, kf, vf, logf, cum = _odd_inproj(x, ng[N_MIX_PRE], w_in_c[o], b_fox_f[o], n_p, tp, ts)
            att_p = _flash_causal(qat, ka, v2t, bp, tp, hps=FOX_HEADS // FOX_KV_HEADS, shared_k=True)
            x = _outproj_residual(x, ng[N_MIX_POST], [att_p], [w_out_c[o]], 0)
            g = FOX_HEADS // FOX_KV_HEADS
            qs = qat[:, n_p:].reshape(FOX_HEADS, LANES, bs, ts)[:, :FOX_HD]
            onehot = jnp.asarray(np.arange(FOX_HEADS)[:, None] // g == np.arange(FOX_KV_HEADS)[None, :], BF16)
            q_bd = jnp.einsum('hdbt,hn->bhtnd', qs, onehot).reshape(bs, FOX_HEADS * ts, FOX_KV_HEADS * FOX_HD)
            cum_s = cum[n_p:, :FOX_HEADS].reshape(bs, ts, FOX_HEADS)
            cq = jnp.swapaxes(cum_s, 1, 2).reshape(bs, FOX_HEADS * ts, 1)
            neg_cq = jnp.broadcast_to(-jnp.swapaxes(cum_s, 1, 2)[:, :, None, :], (bs, FOX_HEADS, ts, ts))
            neg_cq = _pad_axis(neg_cq.reshape(bs, FOX_HEADS * ts, ts), 2, LANES)
            expand = jnp.asarray(np.repeat(np.eye(FOX_HEADS), ts, axis=0), BF16)
            k_new_t = _pad_axis(jnp.swapaxes(kf[n_p:].reshape(bs, ts, -1), 1, 2), 2, PAGE_SIZE)
            v_new_t = _pad_axis(jnp.swapaxes(vf[n_p:].reshape(bs, ts, -1), 1, 2), 2, PAGE_SIZE)
            att_s = _fox_paged(page_table, q_bd, cq, neg_cq, expand, k_new_t, v_new_t, fk_pool_t, fv_pool_t, lf_pool_t, o, ts)
            att_s = jnp.transpose(att_s.reshape(bs, FOX_HEADS, ts, FOX_HD), (0, 2, 1, 3)).reshape(n_s, FOX_HEADS * FOX_HD)
            x = _outproj_residual(x, ng[N_MIX_POST], [att_s], [w_out_c[o]], n_p)
            lf = logf[:, :FOX_HEADS]
            outs["fk_p"].append(kf[:n_p].reshape(bp, tp, FOX_KV_HEADS, FOX_HD))
            outs["fv_p"].append(vf[:n_p].reshape(bp, tp, FOX_KV_HEADS, FOX_HD))
            outs["flf_p"].append(lf[:n_p].reshape(bp, tp, FOX_HEADS))
            outs["fk_s"].append(kf[n_p:].reshape(bs, ts, FOX_KV_HEADS, FOX_HD))
            outs["fv_s"].append(vf[n_p:].reshape(bs, ts, FOX_KV_HEADS, FOX_HD))
            outs["flf_s"].append(lf[n_p:].reshape(bs, ts, FOX_HEADS))
        mk, mv = _norm_matmul(mem_flat, ng[N_MEM_SRC], [w_mem_k[l], w_mem_v[l]], [1.0, 1.0], [F32, F32])
        mk, mv = mk.reshape(bp, m_tok, mem_w), mv.reshape(bp, m_tok, mem_w)
        (mq,) = _norm_matmul(x, ng[N_MEM_PRE], [w_mem_q[l]], [float(MEM_HD ** -0.5)], [F32])
        om_p = _mem_attn(mq[:n_p].reshape(bp, tp, mem_w), mk, mv, 1, _pick_tile(tp, TOKEN_TILE), BF16)
        x = _outproj_residual(x, ng[N_MEM_POST], [om_p.reshape(n_p, mem_w)], [w_mem_o[l]], 0)
        grp = _pick_tile(bs, MEM_GROUP) if bs % 8 == 0 else 1
        om_s = _mem_attn(sample_rows(mq[n_p:], mem_w), cache_mem_k[l].reshape(bs, m_tok, mem_w),
                         cache_mem_v[l].reshape(bs, m_tok, mem_w), grp, 8, F32)
        x = _outproj_residual(x, ng[N_MEM_POST], [om_s[:, :ts].reshape(n_s, mem_w)], [w_mem_o[l]], n_p)
        outs["mk_p"].append(mk.reshape(bp, m_tok, MEM_HEADS, MEM_HD))
        outs["mv_p"].append(mv.reshape(bp, m_tok, MEM_HEADS, MEM_HD))
        x = _ffn_half(x, ng[N_FFN2_PRE], ng[N_FFN2_POST], w_ffn_gate[l, 1], w_ffn_up[l, 1], w_ffn_down[l, 1])

    st = lambda k: jnp.stack(outs[k])
    return (x[:n_p].reshape(bp, tp, d), x[n_p:].reshape(bs, ts, d),
            st("lat_p"), st("kr_p"), st("hg_p"), st("fk_p"), st("fv_p"), st("flf_p"), st("mk_p"), st("mv_p"),
            st("lat_s"), st("kr_s"), st("hg_s"), st("fk_s"), st("fv_s"), st("flf_s"))
```

```python
import functools
import math

import jax
import jax.numpy as jnp
import numpy as np
from jax import lax
from jax.experimental import pallas as pl
from jax.experimental.pallas import tpu as pltpu

F32 = jnp.float32
BF16 = jnp.bfloat16

MLA_HEADS = 8
MLA_NOPE = 64
MLA_ROPE = 32
MLA_V = 64
MLA_KV_LORA = 256
MLA_Q_LORA = 384
ROPE_THETA = 10000.0
HG_HEADS = 4
HG_DK = 128
HG_DV = 128
FOX_HEADS = 16
FOX_KV_HEADS = 4
FOX_HD = 64
MEM_HEADS = 4
MEM_HD = 128
PAGE_SIZE = 128
EPS = 1e-6
N_FFN1_PRE, N_FFN1_POST, N_MIX_PRE, N_MIX_POST, N_MEM_PRE, N_MEM_POST, N_FFN2_PRE, N_FFN2_POST, N_MEM_SRC = range(9)

LANES = 128
VMEM_LIMIT = 56 * 1024 * 1024
TOKEN_TILE = 512
INPROJ_TILE = 256
FFN_TILE = 768
FFN_CHUNK = 512
ATTN_BLOCK = 512
HGRN_BLOCK = 256
HGRN_CHUNK = 16
MLA_FLASH_HEADS = 4
FOX_FLASH_HEADS = 8
MLA_PAGES_PER_STEP = 64
PAGES_PER_STEP = 32
MEM_GROUP = 8


def _params(*sem):
    return pltpu.CompilerParams(dimension_semantics=sem, vmem_limit_bytes=VMEM_LIMIT)


def _rms(x, g):
    return x * lax.rsqrt(jnp.mean(x * x, axis=-1, keepdims=True) + EPS) * g


def _dot(a, b):
    return jnp.dot(a.astype(BF16), b.astype(BF16), preferred_element_type=F32)


def _dot_nt(a, b):
    return lax.dot_general(a.astype(BF16), b.astype(BF16), (((1,), (1,)), ((), ())), preferred_element_type=F32)


def _dot_tn(a, b):
    return lax.dot_general(a.astype(BF16), b.astype(BF16), (((0,), (0,)), ((), ())), preferred_element_type=F32)


def _split3(a):
    hi = a.astype(BF16)
    r = a - hi.astype(F32)
    mid = r.astype(BF16)
    lo = (r - mid.astype(F32)).astype(BF16)
    return hi, mid, lo


def _dot_exact_rhs(m01, a):
    hi, mid, lo = _split3(a)
    return (jnp.dot(m01, hi, preferred_element_type=F32) + jnp.dot(m01, mid, preferred_element_type=F32)
            + jnp.dot(m01, lo, preferred_element_type=F32))


def _dot_exact_lhs(a, m01):
    hi, mid, lo = _split3(a)
    return (jnp.dot(hi, m01, preferred_element_type=F32) + jnp.dot(mid, m01, preferred_element_type=F32)
            + jnp.dot(lo, m01, preferred_element_type=F32))


def _log_sigmoid(x):
    return jnp.minimum(x, 0.0) - jnp.log(1.0 + jnp.exp(-jnp.abs(x)))


def _pick_tile(n, target):
    t = min(n, target)
    while n % t:
        t -= 8
    assert t > 0 and t % 8 == 0, (n, target)
    return t


def _full(shape):
    return pl.BlockSpec(shape, lambda *_: (0,) * len(shape))


def _resident(shape):
    return pl.BlockSpec(shape, lambda *_: (0,) * len(shape), pipeline_mode=pl.Buffered(1))


def _ffn_kernel(x_ref, gpre_ref, gpost_ref, wg_ref, wu_ref, wd_ref, o_ref, acc_ref, *, chunks):
    x = x_ref[...]
    h = _rms(x, gpre_ref[...]).astype(BF16)
    for idx, (c0, cw) in enumerate(chunks):
        g = jnp.dot(h, wg_ref[:, c0:c0 + cw], preferred_element_type=F32)
        u = jnp.dot(h, wu_ref[:, c0:c0 + cw], preferred_element_type=F32)
        a = (g * jax.nn.sigmoid(g) * u).astype(BF16)
        d = jnp.dot(a, wd_ref[c0:c0 + cw, :], preferred_element_type=F32)
        if idx == 0:
            acc_ref[...] = d
        else:
            acc_ref[...] += d
    o_ref[...] = x + 0.5 * _rms(acc_ref[...], gpost_ref[...])


def _ffn_half(x, g_pre, g_post, w_gate, w_up, w_down):
    n, d = x.shape
    f = w_gate.shape[1]
    tm = _pick_tile(n, FFN_TILE)
    chunks = tuple((c0, min(FFN_CHUNK, f - c0)) for c0 in range(0, f, FFN_CHUNK))
    return pl.pallas_call(
        functools.partial(_ffn_kernel, chunks=chunks),
        grid=(n // tm,),
        in_specs=[pl.BlockSpec((tm, d), lambda i: (i, 0)), _full((1, d)), _full((1, d)),
                  _resident((d, f)), _resident((d, f)), _resident((f, d))],
        out_specs=pl.BlockSpec((tm, d), lambda i: (i, 0)),
        out_shape=jax.ShapeDtypeStruct((n, d), F32),
        scratch_shapes=[pltpu.VMEM((tm, d), F32)],
        compiler_params=_params("parallel"),
        name="ffn_half",
    )(x, g_pre.reshape(1, d), g_post.reshape(1, d), w_gate.astype(BF16), w_up.astype(BF16), w_down.astype(BF16))


def _norm_matmul_kernel(x_ref, g_ref, *refs, scales):
    n_out = len(scales)
    h = _rms(x_ref[...], g_ref[...]).astype(BF16)
    for w_ref, o_ref, s in zip(refs[:n_out], refs[n_out:], scales):
        y = jnp.dot(h, w_ref[...], preferred_element_type=F32)
        o_ref[...] = (y if s == 1.0 else y * s).astype(o_ref.dtype)


def _norm_matmul(x, g, ws, scales, dtypes):
    n, d = x.shape
    tm = _pick_tile(n, TOKEN_TILE)
    return pl.pallas_call(
        functools.partial(_norm_matmul_kernel, scales=tuple(scales)),
        grid=(n // tm,),
        in_specs=[pl.BlockSpec((tm, d), lambda i: (i, 0)), _full((1, d))] + [_resident(w.shape) for w in ws],
        out_specs=[pl.BlockSpec((tm, w.shape[1]), lambda i: (i, 0)) for w in ws],
        out_shape=[jax.ShapeDtypeStruct((n, w.shape[1]), dt) for w, dt in zip(ws, dtypes)],
        compiler_params=_params("parallel"),
        name="norm_matmul",
    )(x, g.reshape(1, d), *[w.astype(BF16) for w in ws])


def _outproj_kernel(x_ref, g_ref, *refs, n_parts):
    acc = None
    for a_ref, w_ref in zip(refs[:n_parts], refs[n_parts:2 * n_parts]):
        y = jnp.dot(a_ref[...].astype(BF16), w_ref[...], preferred_element_type=F32)
        acc = y if acc is None else acc + y
    refs[2 * n_parts][...] = x_ref[...] + _rms(acc, g_ref[...])


def _outproj_residual(x, g, parts, ws, row0):
    n, d = x.shape
    rows = parts[0].shape[0]
    tm = _pick_tile(rows, TOKEN_TILE)
    assert row0 % tm == 0
    off = row0 // tm
    return pl.pallas_call(
        functools.partial(_outproj_kernel, n_parts=len(parts)),
        grid=(rows // tm,),
        in_specs=[pl.BlockSpec((tm, d), lambda i: (i + off, 0)), _full((1, d))]
        + [pl.BlockSpec((tm, p.shape[1]), lambda i: (i, 0)) for p in parts] + [_resident(w.shape) for w in ws],
        out_specs=pl.BlockSpec((tm, d), lambda i: (i + off, 0)),
        out_shape=jax.ShapeDtypeStruct((n, d), F32),
        input_output_aliases={0: 0},
        compiler_params=_params("parallel"),
        name="outproj_residual",
    )(x, g.reshape(1, d), *parts, *[w.astype(BF16) for w in ws])


def _even_inproj_kernel(x_ref, gpre_ref, cos_ref, sin_ref, cost_ref, sint_ref, wq_ref, wc_ref, wkr_ref, wh_ref, wuqt_ref,
                        wuk_ref, wuvt_ref, gq_ref, gkv_ref, lb_ref,
                        qt_ref, kp_ref, vt_ref, c_ref, krp_ref, hq_ref, logf_ref, kk_ref, hi_ref, gate_ref, *, q_scale):
    h = _rms(x_ref[...], gpre_ref[...]).astype(BF16)
    cos_t = cos_ref[...]
    sin_t = sin_ref[...]
    cos_tt = cost_ref[...]
    sin_tt = sint_ref[...]
    nh = kp_ref.shape[0]
    qn = _rms(jnp.dot(h, wq_ref[...], preferred_element_type=F32), gq_ref[...]).astype(BF16)
    q2t = _dot_nt(wuqt_ref[...], qn)
    for i in range(nh):
        a = q2t[i * LANES:(i + 1) * LANES, :]
        b = q2t[(nh + i) * LANES:(nh + i + 1) * LANES, :]
        qt_ref[i * LANES:(i + 1) * LANES, :] = ((a * cos_tt + b * sin_tt) * q_scale).astype(qt_ref.dtype)
    c = _rms(jnp.dot(h, wc_ref[...], preferred_element_type=F32), gkv_ref[...])
    c_ref[...] = c
    kr2 = jnp.dot(h, wkr_ref[...], preferred_element_type=F32)
    krp = kr2[:, :LANES] * cos_t + kr2[:, LANES:] * sin_t
    krp_ref[...] = krp
    cb = c.astype(BF16)
    kn = jnp.dot(cb, wuk_ref[...], preferred_element_type=F32)
    for i in range(nh):
        kp_ref[i] = (kn[:, i * LANES:(i + 1) * LANES] + krp).astype(kp_ref.dtype)
    vt_ref[...] = _dot_nt(wuvt_ref[...], cb).astype(vt_ref.dtype)
    hh = jnp.dot(h, wh_ref[...], preferred_element_type=F32)
    w = hq_ref.shape[1]
    lb = lb_ref[...]
    z = hh[:, w:2 * w]
    hg = hh[:, 3 * w:]
    hq_ref[...] = hh[:, :w]
    logf_ref[...] = jnp.log(lb + (1.0 - lb) * jax.nn.sigmoid(z))
    kk_ref[...] = (1.0 - lb) * jax.nn.sigmoid(-z)
    hi_ref[...] = hh[:, 2 * w:3 * w]
    gate_ref[...] = hg * jax.nn.sigmoid(hg)


def _rot_cols(w):
    half = w.shape[-1] // 2
    return jnp.concatenate([-w[..., half:], w[..., :half]], axis=-1)


def _even_inproj(x, g_pre, cos_t, sin_t, cos_tt, sin_tt, w_in, g_q, g_kv, w_uq, w_uk, w_uv, lb, n_prompt):
    n, d = x.shape
    tm = _pick_tile(math.gcd(n_prompt, n - n_prompt), INPROJ_TILE)
    nh, hw = MLA_HEADS, HG_HEADS * HG_DK
    o_q, o_c, o_kr = MLA_Q_LORA, MLA_Q_LORA + MLA_KV_LORA, MLA_Q_LORA + MLA_KV_LORA + MLA_ROPE
    wq, wc, wkr, wh = w_in[:, :o_q], w_in[:, o_q:o_c], w_in[:, o_c:o_kr], w_in[:, o_kr:]
    wkr2 = jnp.zeros((d, 2 * LANES), F32)
    wkr2 = wkr2.at[:, MLA_NOPE:MLA_NOPE + MLA_ROPE].set(wkr).at[:, LANES + MLA_NOPE:LANES + MLA_NOPE + MLA_ROPE].set(_rot_cols(wkr))
    uq = w_uq.reshape(MLA_Q_LORA, nh, MLA_NOPE + MLA_ROPE)
    wuq2 = jnp.zeros((MLA_Q_LORA, 2, nh, LANES), F32)
    wuq2 = wuq2.at[:, 0, :, :MLA_NOPE + MLA_ROPE].set(uq)
    wuq2 = wuq2.at[:, 1, :, MLA_NOPE:MLA_NOPE + MLA_ROPE].set(_rot_cols(uq[..., MLA_NOPE:]))
    wuq2 = wuq2.reshape(MLA_Q_LORA, 2 * nh * LANES)
    wuk = jnp.zeros((MLA_KV_LORA, nh, LANES), F32).at[:, :, :MLA_NOPE].set(w_uk).reshape(MLA_KV_LORA, nh * LANES)
    wuv = jnp.zeros((MLA_KV_LORA, nh // 2, 2, 2, MLA_V), F32)
    uv = w_uv.reshape(MLA_KV_LORA, nh // 2, 2, MLA_V)
    wuv = wuv.at[:, :, 0, 0].set(uv[:, :, 0]).at[:, :, 1, 1].set(uv[:, :, 1]).reshape(MLA_KV_LORA, nh * LANES)
    ws = [w.astype(BF16) for w in (wq, wc, wkr2, wh, wuq2.T, wuk, wuv.T)]
    tok = lambda w: pl.BlockSpec((tm, w), lambda i: (i, 0))
    tok_t = lambda r: pl.BlockSpec((r, tm), lambda i: (0, i))
    head = pl.BlockSpec((nh, tm, LANES), lambda i: (0, i, 0))
    outs = pl.pallas_call(
        functools.partial(_even_inproj_kernel, q_scale=float((MLA_NOPE + MLA_ROPE) ** -0.5)),
        grid=(n // tm,),
        in_specs=[tok(d), _full((1, d)), tok(LANES), tok(LANES), tok_t(LANES), tok_t(LANES)]
        + [_resident(w.shape) for w in ws] + [_full((1, MLA_Q_LORA)), _full((1, MLA_KV_LORA)), _full((1, hw))],
        out_specs=[tok_t(nh * LANES), head, tok_t(nh * LANES), tok(MLA_KV_LORA), tok(LANES)] + [tok(hw)] * 5,
        out_shape=[jax.ShapeDtypeStruct((nh * LANES, n), BF16), jax.ShapeDtypeStruct((nh, n, LANES), BF16),
                   jax.ShapeDtypeStruct((nh * LANES, n), BF16),
                   jax.ShapeDtypeStruct((n, MLA_KV_LORA), F32), jax.ShapeDtypeStruct((n, LANES), F32)]
        + [jax.ShapeDtypeStruct((n, hw), F32)] * 5,
        compiler_params=_params("parallel"),
        name="even_inproj",
    )(x, g_pre.reshape(1, d), cos_t, sin_t, cos_tt, sin_tt, *ws, g_q.reshape(1, -1), g_kv.reshape(1, -1),
      lb.reshape(1, -1))
    return outs


def _odd_inproj_kernel(x_ref, gpre_ref, wqt_ref, wk_ref, wv2t_ref, wkf_ref, wvf_ref, wf_ref, bf_ref,
                       selqt_ref, selk_ref,
                       qat_ref, ka_ref, v2t_ref, kf_ref, vf_ref, logf_ref, cum_ref, carry_ref,
                       *, n_prompt_tiles, tiles_per_seq, sample_shift):
    i = pl.program_id(0)
    tm = x_ref.shape[0]
    is_sample = i >= n_prompt_tiles

    @pl.when(i == 0)
    def _():
        carry_ref[...] = jnp.zeros(carry_ref.shape, F32)

    h = _rms(x_ref[...], gpre_ref[...]).astype(BF16)
    lane = lax.broadcasted_iota(jnp.int32, (tm, LANES), 1)
    fz = jnp.dot(h, wf_ref[...], preferred_element_type=F32) + bf_ref[...]
    logf = jnp.where(lane < FOX_HEADS, _log_sigmoid(fz), 0.0)
    logf_ref[...] = logf
    row = lax.broadcasted_iota(jnp.int32, (tm, tm), 0)
    col = lax.broadcasted_iota(jnp.int32, (tm, tm), 1)
    shift = jnp.where(is_sample, sample_shift, 30)
    tri = jnp.where((col <= row) & ((row >> shift) == (col >> shift)), 1.0, 0.0).astype(BF16)
    fresh = jnp.logical_or(is_sample, i % tiles_per_seq == 0)
    carry = jnp.where(fresh, 0.0, carry_ref[...])
    cum = _dot_exact_rhs(tri, logf) + carry
    cum_ref[...] = cum
    carry_ref[...] = cum[tm - 1:tm, :]
    hi, mid, lo = _split3(cum)
    hi = jnp.where(lane == FOX_HEADS, 1.0, hi)
    qat_ref[...] = (_dot_nt(wqt_ref[...], h) + _dot_nt(selqt_ref[0], hi) + _dot_nt(selqt_ref[1], mid)
                    + _dot_nt(selqt_ref[2], lo)).astype(qat_ref.dtype)
    ka = (jnp.dot(h, wk_ref[...], preferred_element_type=F32) + jnp.dot(hi, selk_ref[0], preferred_element_type=F32)
          + jnp.dot(mid, selk_ref[1], preferred_element_type=F32) + jnp.dot(lo, selk_ref[2], preferred_element_type=F32))
    for j in range(ka_ref.shape[0]):
        ka_ref[j] = ka[:, j * LANES:(j + 1) * LANES].astype(ka_ref.dtype)
    v2t_ref[...] = _dot_nt(wv2t_ref[...], h).astype(v2t_ref.dtype)
    kf_ref[...] = jnp.dot(h, wkf_ref[...], preferred_element_type=F32)
    vf_ref[...] = jnp.dot(h, wvf_ref[...], preferred_element_type=F32)


def _fox_selectors():
    nq, nk, g = FOX_HEADS, FOX_KV_HEADS, FOX_HEADS // FOX_KV_HEADS
    selq = np.zeros((3, LANES, nq * LANES), np.float32)
    selk = np.zeros((3, LANES, nk * LANES), np.float32)
    base = FOX_HD
    ones_row = FOX_HEADS
    for hd in range(nq):
        n, gi = divmod(hd, g)
        for t in range(3):
            selq[t, hd, hd * LANES + base + t] = 1.0
            selq[0, ones_row, hd * LANES + base + 3 + 3 * gi + t] = 1.0
            selk[t, hd, n * LANES + base + 3 + 3 * gi + t] = -1.0
    for n in range(nk):
        selk[0, ones_row, n * LANES + base:n * LANES + base + 3] = 1.0
    return jnp.asarray(np.swapaxes(selq, 1, 2), BF16), jnp.asarray(selk, BF16)


def _odd_inproj(x, g_pre, w_in, b_f, n_prompt, seq_prompt, seq_sample):
    n, d = x.shape
    tm = _pick_tile(math.gcd(math.gcd(n_prompt, n - n_prompt), seq_prompt), INPROJ_TILE)
    assert n_prompt % tm == 0 and seq_prompt % tm == 0 and tm % seq_sample == 0
    assert seq_sample & (seq_sample - 1) == 0, "sample sequences must be a power of two long"
    nq, nk, hd = FOX_HEADS, FOX_KV_HEADS, FOX_HD
    o_q, o_k, o_v = nq * hd, nq * hd + nk * hd, nq * hd + 2 * nk * hd
    w_q, w_k, w_v, w_f = w_in[:, :o_q], w_in[:, o_q:o_k], w_in[:, o_k:o_v], w_in[:, o_v:]
    wq = jnp.zeros((d, nq, LANES), F32).at[:, :, :hd].set(w_q.reshape(d, nq, hd) * float(hd ** -0.5)).reshape(d, nq * LANES)
    wk = jnp.zeros((d, nk, LANES), F32).at[:, :, :hd].set(w_k.reshape(d, nk, hd)).reshape(d, nk * LANES)
    wv2 = jnp.zeros((d, nk, 2, 2, hd), F32)
    wv2 = wv2.at[:, :, 0, 0].set(w_v.reshape(d, nk, hd)).at[:, :, 1, 1].set(w_v.reshape(d, nk, hd)).reshape(d, 2 * nk * LANES)
    wf = jnp.zeros((d, LANES), F32).at[:, :nq].set(w_f)
    bf = jnp.zeros((1, LANES), F32).at[0, :nq].set(b_f)
    selq_t, selk = _fox_selectors()
    ws = [w.astype(BF16) for w in (wq.T, wk, wv2.T, w_k, w_v, wf)]
    tok = lambda w: pl.BlockSpec((tm, w), lambda i: (i, 0))
    tok_t = lambda r: pl.BlockSpec((r, tm), lambda i: (0, i))
    head = lambda k: pl.BlockSpec((k, tm, LANES), lambda i: (0, i, 0))
    return pl.pallas_call(
        functools.partial(_odd_inproj_kernel, n_prompt_tiles=n_prompt // tm, tiles_per_seq=seq_prompt // tm,
                          sample_shift=int(seq_sample).bit_length() - 1),
        grid=(n // tm,),
        in_specs=[tok(d), _full((1, d))] + [_resident(w.shape) for w in ws]
        + [_full((1, LANES)), _resident(selq_t.shape), _resident(selk.shape)],
        out_specs=[tok_t(nq * LANES), head(nk), tok_t(2 * nk * LANES), tok(nk * hd), tok(nk * hd), tok(LANES), tok(LANES)],
        out_shape=[jax.ShapeDtypeStruct((nq * LANES, n), BF16), jax.ShapeDtypeStruct((nk, n, LANES), BF16),
                   jax.ShapeDtypeStruct((2 * nk * LANES, n), BF16), jax.ShapeDtypeStruct((n, nk * hd), F32),
                   jax.ShapeDtypeStruct((n, nk * hd), F32), jax.ShapeDtypeStruct((n, LANES), F32),
                   jax.ShapeDtypeStruct((n, LANES), F32)],
        scratch_shapes=[pltpu.VMEM((1, LANES), F32)],
        compiler_params=_params("arbitrary"),
        name="odd_inproj",
    )(x, g_pre.reshape(1, d), *ws, bf, selq_t, selk)


def _flash_kernel(qi_ref, kj_ref, qt_ref, k_ref, vt_ref, o_ref, m_ref, l_ref, acc_ref, *, hps, hpk):
    p = pl.program_id(2)
    i = qi_ref[p]
    j = kj_ref[p]
    tk, tq = k_ref.shape[1], qt_ref.shape[1]

    @pl.when(j == 0)
    def _():
        m_ref[...] = jnp.full(m_ref.shape, -jnp.inf, F32)
        l_ref[...] = jnp.zeros(l_ref.shape, F32)
        acc_ref[...] = jnp.zeros(acc_ref.shape, F32)

    def update(masked):
        if masked:
            keep = lax.broadcasted_iota(jnp.int32, (tk, tq), 0) <= lax.broadcasted_iota(jnp.int32, (tk, tq), 1)
        for hh in range(hps):
            vi = hh if hpk == 1 else (hh // hpk) * 2 + hh % 2
            s = jnp.dot(k_ref[hh // hpk], qt_ref[hh * LANES:(hh + 1) * LANES, :],
                        preferred_element_type=F32)
            if masked:
                s = jnp.where(keep, s, -jnp.inf)
            m_prev = m_ref[hh]
            m_new = jnp.maximum(m_prev, jnp.max(s, axis=0, keepdims=True))
            alpha = jnp.exp(m_prev - m_new)
            pe = jnp.exp(s - m_new)
            l_ref[hh] = alpha * l_ref[hh] + jnp.sum(pe, axis=0, keepdims=True)
            vt = vt_ref[vi * LANES:(vi + 1) * LANES, :]
            acc_ref[hh] = alpha * acc_ref[hh] + jnp.dot(vt, pe.astype(BF16), preferred_element_type=F32)
            m_ref[hh] = m_new

    @pl.when(j < i)
    def _():
        update(False)

    @pl.when(j == i)
    def _():
        update(True)
        for r in range(hps // 2):
            ot = acc_ref[2 * r] / l_ref[2 * r] + acc_ref[2 * r + 1] / l_ref[2 * r + 1]
            o_ref[:, r * LANES:(r + 1) * LANES] = ot.T.astype(o_ref.dtype)


def _flash_causal(qt, k, vt, n_seq, seq_len, hps, hpk):
    nh = qt.shape[0] // LANES
    groups = nh // hps
    kh = hps // hpk
    vrows = (hps if hpk == 1 else 2 * kh) * LANES
    blk = _pick_tile(seq_len, ATTN_BLOCK)
    nb = seq_len // blk
    pairs = [(a, b) for a in range(nb) for b in range(a + 1)]
    qi = jnp.asarray([a for a, _ in pairs], jnp.int32)
    kj = jnp.asarray([b for _, b in pairs], jnp.int32)
    gs = pltpu.PrefetchScalarGridSpec(
        num_scalar_prefetch=2, grid=(n_seq, groups, len(pairs)),
        in_specs=[pl.BlockSpec((hps * LANES, blk), lambda b, g, p, qi, kj: (g, b * nb + qi[p])),
                  pl.BlockSpec((kh, blk, LANES), lambda b, g, p, qi, kj: (g, b * nb + kj[p], 0)),
                  pl.BlockSpec((vrows, blk), lambda b, g, p, qi, kj: (g, b * nb + kj[p]))],
        out_specs=pl.BlockSpec((blk, hps * LANES // 2), lambda b, g, p, qi, kj: (b * nb + qi[p], g)),
        scratch_shapes=[pltpu.VMEM((hps, 1, blk), F32), pltpu.VMEM((hps, 1, blk), F32), pltpu.VMEM((hps, LANES, blk), F32)])
    return pl.pallas_call(
        functools.partial(_flash_kernel, hps=hps, hpk=hpk),
        grid_spec=gs,
        out_shape=jax.ShapeDtypeStruct((n_seq * seq_len, nh * LANES // 2), BF16),
        compiler_params=_params("parallel", "parallel", "arbitrary"),
        name="flash_causal",
    )(qi, kj, qt, k, vt)


def _hgrn_kernel(hq_ref, kk_ref, hi_ref, logf_ref, gate_ref, s0_ref, gout_ref, o_ref, s_ref, obuf_ref, *, chunk):
    j = pl.program_id(1)
    tb = hq_ref.shape[0]
    nh = s_ref.shape[1]
    dk = s_ref.shape[2]

    @pl.when(j == 0)
    def _():
        s_ref[...] = s0_ref[...]

    row = lax.broadcasted_iota(jnp.int32, (chunk, chunk), 0)
    col = lax.broadcasted_iota(jnp.int32, (chunk, chunk), 1)
    causal = col <= row
    tri = jnp.where(causal, 1.0, 0.0).astype(BF16)

    def step(c, carry):
        r0 = pl.multiple_of(c * chunk, chunk)
        for hh in range(nh):
            sl = (pl.ds(r0, chunk), slice(hh * dk, (hh + 1) * dk))
            q, k, v, g = hq_ref[sl], kk_ref[sl], hi_ref[sl], logf_ref[sl]
            cum = _dot_exact_rhs(tri, g)
            a = jnp.zeros((chunk, chunk), F32)
            for s in range(chunk):
                w = q * k[s:s + 1, :] * jnp.exp(jnp.minimum(cum - cum[s:s + 1, :], 0.0))
                a = jnp.where(col == s, jnp.sum(w, axis=-1, keepdims=True), a)
            a = jnp.where(causal, a, 0.0)
            state = s_ref[0, hh]
            o = _dot(a, v) + _dot(q * jnp.exp(cum), state)
            obuf_ref[sl] = o
            last = cum[chunk - 1:chunk, :]
            decay = jnp.broadcast_to(jnp.exp(last), (8, dk)).T[:, 0:1]
            s_ref[0, hh] = decay * state + _dot_tn(k * jnp.exp(last - cum), v)
        return carry

    lax.fori_loop(0, tb // chunk, step, 0)
    for hh in range(nh):
        sl = (slice(None), slice(hh * dk, (hh + 1) * dk))
        o_ref[sl] = (_rms(obuf_ref[sl], gout_ref[sl]) * gate_ref[sl]).astype(o_ref.dtype)


def _hgrn(hq, kk, hi, logf, gate, s0, g_out, n_seq, seq_len, block, chunk, dtype):
    w = hq.shape[1]
    nb = seq_len // block
    tok = pl.BlockSpec((block, w), lambda s, j: (s * nb + j, 0))
    st = pl.BlockSpec((1,) + s0.shape[1:], lambda s, j: (s, 0, 0, 0))
    return pl.pallas_call(
        functools.partial(_hgrn_kernel, chunk=chunk),
        grid=(n_seq, nb),
        in_specs=[tok] * 5 + [st, _full((1, w))],
        out_specs=[tok, st],
        out_shape=[jax.ShapeDtypeStruct((n_seq * seq_len, w), dtype), jax.ShapeDtypeStruct(s0.shape, F32)],
        scratch_shapes=[pltpu.VMEM((block, w), F32)],
        compiler_params=_params("parallel", "arbitrary"),
        name="hgrn2",
    )(hq, kk, hi, logf, gate, s0, g_out.reshape(1, w))


def _headmm_kernel(x_ref, w_ref, o_ref):
    o_ref[...] = jnp.dot(x_ref[...].astype(BF16), w_ref[...], preferred_element_type=F32).astype(o_ref.dtype)


def _headmm(x, w, rows, row0, dtype):
    nh, _, kdim = x.shape
    assert row0 % rows == 0
    off = row0 // rows
    return pl.pallas_call(
        _headmm_kernel,
        grid=(nh,),
        in_specs=[pl.BlockSpec((None, rows, kdim), lambda h: (h, off, 0)),
                  pl.BlockSpec((None,) + w.shape[1:], lambda h: (h, 0, 0))],
        out_specs=pl.BlockSpec((None, rows, w.shape[2]), lambda h: (h, 0, 0)),
        out_shape=jax.ShapeDtypeStruct((nh, rows, w.shape[2]), dtype),
        compiler_params=_params("parallel"),
        name="headmm",
    )(x, w.astype(BF16))


def _online_update(m_ref, l_ref, acc_ref, s, pv):
    m_prev = m_ref[...]
    m_new = jnp.maximum(m_prev, jnp.max(s, axis=-1, keepdims=True))
    alpha = jnp.exp(m_prev - m_new)
    pe = jnp.exp(s - m_new)
    l_ref[...] = alpha * l_ref[...] + jnp.sum(pe, axis=-1, keepdims=True)
    acc_ref[...] = alpha * acc_ref[...] + pv(pe.astype(BF16))
    m_ref[...] = m_new


def _new_token_mask(rows, t_new):
    r = lax.broadcasted_iota(jnp.int32, (rows, LANES), 0)
    c = lax.broadcasted_iota(jnp.int32, (rows, LANES), 1)
    return (c < t_new) & (c <= (r & (t_new - 1)))


def _page_copies(pt_ref, pools, bufs, sems, b, j, slot, *, layer, pps, n_pages, latest_first):
    out = []
    for i in range(pps):
        idx = j * pps + i
        pg = pt_ref[b, (n_pages - 1 - idx) if latest_first else idx]
        for k, (pool, buf) in enumerate(zip(pools, bufs)):
            out.append(pltpu.make_async_copy(pool.at[layer, pg], buf.at[slot, i], sems.at[k, slot]))
    return out


def _gather_pages(pt_ref, pools, bufs, sems, **kw):
    b, j = pl.program_id(0), pl.program_id(1)
    nb, nj = pl.num_programs(0), pl.num_programs(1)
    t = b * nj + j
    slot = lax.rem(t, 2)
    copies = functools.partial(_page_copies, pt_ref, pools, bufs, sems, **kw)

    @pl.when(t == 0)
    def _():
        for c in copies(b, j, slot):
            c.start()

    @pl.when(t + 1 < nb * nj)
    def _():
        t1 = t + 1
        for c in copies(lax.div(t1, nj), lax.rem(t1, nj), 1 - slot):
            c.start()

    for c in copies(b, j, slot):
        c.wait()
    return slot


def _mla_paged_kernel(pt_ref, qa_ref, qr_ref, cnew_ref, krnew_ref, lat_hbm, kr_hbm, o_ref,
                      ck_buf, kr_buf, sems, m_ref, l_ref, acc_ref, *, pps, t_new, layer, n_pages):
    j = pl.program_id(1)
    rows = qa_ref.shape[0]
    slot = _gather_pages(pt_ref, (lat_hbm, kr_hbm), (ck_buf, kr_buf), sems, layer=layer, pps=pps, n_pages=n_pages,
                         latest_first=False)

    @pl.when(j == 0)
    def _():
        m_ref[...] = jnp.full(m_ref.shape, -jnp.inf, F32)
        l_ref[...] = jnp.zeros(l_ref.shape, F32)
        acc_ref[...] = jnp.zeros(acc_ref.shape, F32)

    qa = qa_ref[...]
    qr = qr_ref[...]
    cks = [ck_buf[slot, i].astype(BF16) for i in range(pps)]
    s = jnp.concatenate([_dot_nt(qa, ck) + _dot(qr, kr_buf[slot, i]) for i, ck in enumerate(cks)], axis=-1)

    def pv(p):
        out = None
        for idx, ck in enumerate(cks):
            y = jnp.dot(p[:, idx * PAGE_SIZE:(idx + 1) * PAGE_SIZE], ck, preferred_element_type=F32)
            out = y if out is None else out + y
        return out

    _online_update(m_ref, l_ref, acc_ref, s, pv)

    @pl.when(j == pl.num_programs(1) - 1)
    def _():
        cn = cnew_ref[...].astype(BF16)
        sn = _dot_nt(qa, cn) + _dot(qr, krnew_ref[...])
        sn = jnp.where(_new_token_mask(rows, t_new), sn, -jnp.inf)
        _online_update(m_ref, l_ref, acc_ref, sn, lambda p: jnp.dot(p, cn, preferred_element_type=F32))
        o_ref[...] = (acc_ref[...] / l_ref[...]).astype(o_ref.dtype)


def _mla_paged(page_table, qa, qr, c_new, kr_new_t, lat_pool, kr_pool_t, layer, t_new):
    bsz, rows, cdim = qa.shape
    n_pages = page_table.shape[1]
    pps = min(MLA_PAGES_PER_STEP, n_pages)
    assert n_pages % pps == 0 and t_new & (t_new - 1) == 0
    rdim = qr.shape[2]
    per_b = lambda shape: pl.BlockSpec((None,) + shape, lambda b, j, pt: (b, 0, 0))
    hbm = pl.BlockSpec(memory_space=pl.ANY)
    gs = pltpu.PrefetchScalarGridSpec(
        num_scalar_prefetch=1, grid=(bsz, n_pages // pps),
        in_specs=[per_b((rows, cdim)), per_b((rows, rdim)), per_b((PAGE_SIZE, cdim)), per_b((rdim, PAGE_SIZE)), hbm, hbm],
        out_specs=per_b((rows, cdim)),
        scratch_shapes=[pltpu.VMEM((2, pps, PAGE_SIZE, cdim), F32), pltpu.VMEM((2, pps, rdim, PAGE_SIZE), F32),
                        pltpu.SemaphoreType.DMA((2, 2)),
                        pltpu.VMEM((rows, 1), F32), pltpu.VMEM((rows, 1), F32), pltpu.VMEM((rows, cdim), F32)])
    return pl.pallas_call(
        functools.partial(_mla_paged_kernel, pps=pps, t_new=t_new, layer=layer, n_pages=n_pages),
        grid_spec=gs,
        out_shape=jax.ShapeDtypeStruct((bsz, rows, cdim), F32),
        compiler_params=_params("arbitrary", "arbitrary"),
        name="mla_paged",
    )(page_table, qa, qr, c_new, kr_new_t, lat_pool, kr_pool_t)


def _fox_paged_kernel(pt_ref, q_ref, cq_ref, ncq_ref, e_ref, knew_ref, vnew_ref, k_hbm, v_hbm, lf_hbm, o_ref,
                      k_buf, v_buf, lf_buf, sems, m_ref, l_ref, acc_ref, suf_ref, *, pps, t_new, n_kv, layer, n_pages):
    j = pl.program_id(1)
    rows = q_ref.shape[0]
    slot = _gather_pages(pt_ref, (k_hbm, v_hbm, lf_hbm), (k_buf, v_buf, lf_buf), sems, layer=layer, pps=pps,
                         n_pages=n_pages, latest_first=True)

    @pl.when(j == 0)
    def _():
        m_ref[...] = jnp.full(m_ref.shape, -jnp.inf, F32)
        l_ref[...] = jnp.zeros(l_ref.shape, F32)
        acc_ref[...] = jnp.zeros(acc_ref.shape, F32)
        suf_ref[...] = jnp.zeros(suf_ref.shape, F32)

    q = q_ref[...]
    cq = cq_ref[...]
    e = e_ref[...]
    r_i = lax.broadcasted_iota(jnp.int32, (PAGE_SIZE, PAGE_SIZE), 0)
    c_i = lax.broadcasted_iota(jnp.int32, (PAGE_SIZE, PAGE_SIZE), 1)
    later = jnp.where(r_i > c_i, 1.0, 0.0).astype(BF16)
    run = suf_ref[...]
    ss = []
    for i in range(pps):
        lf = _dot_exact_rhs(e, lf_buf[slot, i])
        bias = _dot_exact_lhs(lf, later) + (run + cq)
        run = run + jnp.sum(lf, axis=-1, keepdims=True)
        ss.append(_dot(q, k_buf[slot, i]) + bias)
    suf_ref[...] = run
    s = jnp.concatenate(ss, axis=-1)

    def pv(p):
        out = None
        for i in range(pps):
            y = _dot_nt(p[:, i * PAGE_SIZE:(i + 1) * PAGE_SIZE], v_buf[slot, i])
            out = y if out is None else out + y
        return out

    _online_update(m_ref, l_ref, acc_ref, s, pv)

    @pl.when(j == pl.num_programs(1) - 1)
    def _():
        sn = _dot(q, knew_ref[...]) + cq + ncq_ref[...]
        sn = jnp.where(_new_token_mask(rows, t_new), sn, -jnp.inf)
        vn = vnew_ref[...]
        _online_update(m_ref, l_ref, acc_ref, sn, lambda p: _dot_nt(p, vn))
        rk = rows // n_kv
        hd = acc_ref.shape[1] // n_kv
        for n in range(n_kv):
            o_ref[n * rk:(n + 1) * rk, :] = (acc_ref[n * rk:(n + 1) * rk, n * hd:(n + 1) * hd]
                                             / l_ref[n * rk:(n + 1) * rk, :]).astype(o_ref.dtype)


def _fox_paged(page_table, q_bd, cq, neg_cq, expand, k_new_t, v_new_t, k_pool_t, v_pool_t, lf_pool_t, layer, t_new):
    bsz, rows, kd = q_bd.shape
    n_pages = page_table.shape[1]
    nh = lf_pool_t.shape[2]
    pps = min(PAGES_PER_STEP, n_pages)
    assert n_pages % pps == 0 and t_new & (t_new - 1) == 0
    per_b = lambda shape: pl.BlockSpec((None,) + shape, lambda b, j, pt: (b, 0, 0))
    hbm = pl.BlockSpec(memory_space=pl.ANY)
    gs = pltpu.PrefetchScalarGridSpec(
        num_scalar_prefetch=1, grid=(bsz, n_pages // pps),
        in_specs=[per_b((rows, kd)), per_b((rows, 1)), per_b((rows, LANES)),
                  pl.BlockSpec((rows, nh), lambda b, j, pt: (0, 0)), per_b((kd, PAGE_SIZE)), per_b((kd, PAGE_SIZE)),
                  hbm, hbm, hbm],
        out_specs=per_b((rows, kd // FOX_KV_HEADS)),
        scratch_shapes=[pltpu.VMEM((2, pps, kd, PAGE_SIZE), F32), pltpu.VMEM((2, pps, kd, PAGE_SIZE), F32),
                        pltpu.VMEM((2, pps, nh, PAGE_SIZE), F32), pltpu.SemaphoreType.DMA((3, 2)),
                        pltpu.VMEM((rows, 1), F32), pltpu.VMEM((rows, 1), F32), pltpu.VMEM((rows, kd), F32),
                        pltpu.VMEM((rows, 1), F32)])
    return pl.pallas_call(
        functools.partial(_fox_paged_kernel, pps=pps, t_new=t_new, n_kv=FOX_KV_HEADS, layer=layer, n_pages=n_pages),
        grid_spec=gs,
        out_shape=jax.ShapeDtypeStruct((bsz, rows, kd // FOX_KV_HEADS), F32),
        compiler_params=_params("arbitrary", "arbitrary"),
        name="fox_paged",
    )(page_table, q_bd, cq, neg_cq, expand, k_new_t, v_new_t, k_pool_t, v_pool_t, lf_pool_t)


def _mem_attn_kernel(q_ref, k_ref, v_ref, o_ref, *, n_heads):
    hd = q_ref.shape[2] // n_heads
    for b in range(q_ref.shape[0]):
        for hh in range(n_heads):
            sl = slice(hh * hd, (hh + 1) * hd)
            s = _dot_nt(q_ref[b, :, sl], k_ref[b, :, sl])
            pe = jnp.exp(s - jnp.max(s, axis=-1, keepdims=True))
            p = (pe / jnp.sum(pe, axis=-1, keepdims=True)).astype(BF16)
            o_ref[b, :, sl] = _dot(p, v_ref[b, :, sl]).astype(o_ref.dtype)


def _mem_sample_kernel(qt_ref, k_ref, v_ref, o_ref, *, n_heads):
    cols = qt_ref.shape[2]
    tpad = cols // n_heads
    rows = k_ref.shape[1]
    r = lax.broadcasted_iota(jnp.int32, (rows, cols), 0)
    c = lax.broadcasted_iota(jnp.int32, (rows, cols), 1)
    keep = (r % n_heads) == (c // tpad)
    for b in range(qt_ref.shape[0]):
        st = jnp.where(keep, _dot(k_ref[b], qt_ref[b]), -jnp.inf)
        pe = jnp.exp(st - jnp.max(st, axis=0, keepdims=True))
        p = pe / jnp.sum(pe, axis=0, keepdims=True)
        o_ref[b] = _dot_tn(p, v_ref[b]).astype(o_ref.dtype)


def _mem_attn_sample(qt, mk, mv, group):
    nb, d, cols = qt.shape
    rows = mk.shape[1]
    return pl.pallas_call(
        functools.partial(_mem_sample_kernel, n_heads=MEM_HEADS),
        grid=(nb // group,),
        in_specs=[pl.BlockSpec((group, d, cols), lambda g: (g, 0, 0)),
                  pl.BlockSpec((group, rows, d), lambda g: (g, 0, 0)),
                  pl.BlockSpec((group, rows, d), lambda g: (g, 0, 0))],
        out_specs=pl.BlockSpec((group, cols, d), lambda g: (g, 0, 0)),
        out_shape=jax.ShapeDtypeStruct((nb, cols, d), F32),
        compiler_params=_params("parallel"),
        name="mem_attn_sample",
    )(qt, mk, mv)


def _mem_attn(q, mk, mv, group, tq, dtype):
    nb, t, w = q.shape
    m = mk.shape[1]
    return pl.pallas_call(
        functools.partial(_mem_attn_kernel, n_heads=MEM_HEADS),
        grid=(nb // group, t // tq),
        in_specs=[pl.BlockSpec((group, tq, w), lambda g, i: (g, i, 0)),
                  pl.BlockSpec((group, m, w), lambda g, i: (g, 0, 0)),
                  pl.BlockSpec((group, m, w), lambda g, i: (g, 0, 0))],
        out_specs=pl.BlockSpec((group, tq, w), lambda g, i: (g, i, 0)),
        out_shape=jax.ShapeDtypeStruct((nb, t, w), dtype),
        compiler_params=_params("parallel", "parallel"),
        name="mem_attn",
    )(q, mk, mv)


def _rope_tables(pos):
    half = MLA_ROPE // 2
    inv_freq = ROPE_THETA ** (-jnp.arange(half, dtype=F32) / half)
    ang = pos.astype(F32)[:, None] * inv_freq[None, :]
    cos, sin = jnp.cos(ang), jnp.sin(ang)
    n = pos.shape[0]
    pad = jnp.zeros((n, LANES - MLA_NOPE - MLA_ROPE), F32)
    cos_t = jnp.concatenate([jnp.ones((n, MLA_NOPE), F32), cos, cos, pad], axis=-1)
    sin_t = jnp.concatenate([jnp.zeros((n, MLA_NOPE), F32), sin, sin, pad], axis=-1)
    return cos_t, sin_t


def _pad_axis(a, axis, size):
    pad = [(0, 0)] * a.ndim
    pad[axis] = (0, size - a.shape[axis])
    return jnp.pad(a, pad)


def kernel(x_prompt, x_sample, mem_prompt, cache_mla_latent, cache_mla_krope, state_hgrn, cache_fox_k, cache_fox_v, cache_fox_logf, cache_mem_k, cache_mem_v, page_table, norm_gains, w_ffn_gate, w_ffn_up, w_ffn_down, w_mem_q, w_mem_k, w_mem_v, w_mem_o, w_in_ab, g_mla_q, g_mla_kv, w_mla_uq, w_mla_uk, w_mla_uv, hgrn_lb_logits, g_hgrn_out, w_out_ab, w_in_c, b_fox_f, w_out_c):
    bp, tp, d = x_prompt.shape
    bs, ts, _ = x_sample.shape
    n_p, n_s = bp * tp, bs * ts
    depth = norm_gains.shape[0]
    n_pages = page_table.shape[1]
    m_tok = mem_prompt.shape[1]
    mem_w = MEM_HEADS * MEM_HD
    hg_w = HG_HEADS * HG_DK
    assert ts <= 8 and ts & (ts - 1) == 0

    x = jnp.concatenate([x_prompt.reshape(n_p, d), x_sample.reshape(n_s, d)], axis=0)
    pos = jnp.concatenate([jnp.tile(jnp.arange(tp), bp), n_pages * PAGE_SIZE + jnp.tile(jnp.arange(ts), bs)])
    cos_t, sin_t = _rope_tables(pos)
    cos_tt, sin_tt = cos_t.T, sin_t.T
    lb_all = jnp.cumsum(jax.nn.softmax(hgrn_lb_logits.astype(F32), axis=0), axis=0)
    kr_pool_t = jnp.swapaxes(cache_mla_krope, 2, 3)
    fk_pool_t = jnp.transpose(cache_fox_k, (0, 1, 3, 4, 2)).reshape(cache_fox_k.shape[0], -1, FOX_KV_HEADS * FOX_HD, PAGE_SIZE)
    fv_pool_t = jnp.transpose(cache_fox_v, (0, 1, 3, 4, 2)).reshape(cache_fox_v.shape[0], -1, FOX_KV_HEADS * FOX_HD, PAGE_SIZE)
    lf_pool_t = jnp.swapaxes(cache_fox_logf, 2, 3)
    mem_flat = mem_prompt.reshape(bp * m_tok, d)

    outs = {k: [] for k in ("lat_p", "kr_p", "hg_p", "fk_p", "fv_p", "flf_p", "mk_p", "mv_p",
                            "lat_s", "kr_s", "hg_s", "fk_s", "fv_s", "flf_s")}

    def sample_rows(a, width):
        return _pad_axis(a.reshape(bs, ts, width), 1, 8)

    for l in range(depth):
        ng = norm_gains[l]
        x = _ffn_half(x, ng[N_FFN1_PRE], ng[N_FFN1_POST], w_ffn_gate[l, 0], w_ffn_up[l, 0], w_ffn_down[l, 0])
        if l % 2 == 0:
            e = l // 2
            qt, kp, vt, c, krp, hq, logf, kk, hi, gate = _even_inproj(
                x, ng[N_MIX_PRE], cos_t, sin_t, cos_tt, sin_tt, w_in_ab[e], g_mla_q[e], g_mla_kv[e], w_mla_uq[e],
                w_mla_uk[e], w_mla_uv[e], lb_all[e], n_p)
            kr = krp[:, MLA_NOPE:MLA_NOPE + MLA_ROPE]
            oa_p = _flash_causal(qt, kp, vt, bp, tp, hps=MLA_FLASH_HEADS, hpk=1)
            ob_p, s_p = _hgrn(hq, kk, hi, logf, gate, jnp.zeros((bp, HG_HEADS, HG_DK, HG_DV), F32), g_hgrn_out[e],
                              bp, tp, _pick_tile(tp, HGRN_BLOCK), HGRN_CHUNK, BF16)
            w_o = w_out_ab[e]
            x = _outproj_residual(x, ng[N_MIX_POST], [oa_p, ob_p], [w_o[:MLA_HEADS * MLA_V], w_o[MLA_HEADS * MLA_V:]], 0)
            w_uk_t = jnp.zeros((MLA_HEADS, LANES, MLA_KV_LORA), F32).at[:, :MLA_NOPE].set(jnp.transpose(w_mla_uk[e], (1, 2, 0)))
            q_s = jnp.swapaxes(qt[:, n_p:].reshape(MLA_HEADS, LANES, n_s), 1, 2)
            qa = _headmm(q_s, w_uk_t, n_s, 0, BF16)
            qa = jnp.transpose(qa.reshape(MLA_HEADS, bs, ts, MLA_KV_LORA), (1, 0, 2, 3)).reshape(bs, MLA_HEADS * ts, MLA_KV_LORA)
            qr = q_s[:, :, MLA_NOPE:MLA_NOPE + MLA_ROPE].reshape(MLA_HEADS, bs, ts, MLA_ROPE)
            qr = jnp.transpose(qr, (1, 0, 2, 3)).reshape(bs, MLA_HEADS * ts, MLA_ROPE)
            c_new = _pad_axis(c[n_p:].reshape(bs, ts, MLA_KV_LORA), 1, PAGE_SIZE)
            kr_new_t = _pad_axis(jnp.swapaxes(kr[n_p:].reshape(bs, ts, MLA_ROPE), 1, 2), 2, PAGE_SIZE)
            att = _mla_paged(page_table, qa, qr, c_new, kr_new_t, cache_mla_latent, kr_pool_t, e, ts)
            att = jnp.transpose(att.reshape(bs, MLA_HEADS, ts, MLA_KV_LORA), (1, 0, 2, 3)).reshape(MLA_HEADS, n_s, MLA_KV_LORA)
            oa_s = _headmm(att, jnp.transpose(w_mla_uv[e], (1, 0, 2)), n_s, 0, BF16)
            oa_s = jnp.transpose(oa_s, (1, 0, 2)).reshape(n_s, MLA_HEADS * MLA_V)
            pads = [sample_rows(a[n_p:], hg_w).reshape(bs * 8, hg_w) for a in (hq, kk, hi, logf, gate)]
            ob_s, s_s = _hgrn(*pads, state_hgrn[e], g_hgrn_out[e], bs, 8, 8, 8, F32)
            ob_s = ob_s.reshape(bs, 8, hg_w)[:, :ts].reshape(n_s, hg_w)
            x = _outproj_residual(x, ng[N_MIX_POST], [oa_s, ob_s], [w_o[:MLA_HEADS * MLA_V], w_o[MLA_HEADS * MLA_V:]], n_p)
            outs["lat_p"].append(c[:n_p].reshape(bp, tp, MLA_KV_LORA))
            outs["kr_p"].append(kr[:n_p].reshape(bp, tp, MLA_ROPE))
            outs["hg_p"].append(s_p)
            outs["lat_s"].append(c[n_p:].reshape(bs, ts, MLA_KV_LORA))
            outs["kr_s"].append(kr[n_p:].reshape(bs, ts, MLA_ROPE))
            outs["hg_s"].append(s_s)
        else:
            o = l // 2
            qat, ka, v2t, kf, vf, logf, cum = _odd_inproj(x, ng[N_MIX_PRE], w_in_c[o], b_fox_f[o], n_p, tp, ts)
            att_p = _flash_causal(qat, ka, v2t, bp, tp, hps=FOX_FLASH_HEADS, hpk=FOX_HEADS // FOX_KV_HEADS)
            x = _outproj_residual(x, ng[N_MIX_POST], [att_p], [w_out_c[o]], 0)
            g = FOX_HEADS // FOX_KV_HEADS
            qs = qat[:, n_p:].reshape(FOX_HEADS, LANES, bs, ts)[:, :FOX_HD]
            onehot = jnp.asarray(np.arange(FOX_HEADS)[:, None] // g == np.arange(FOX_KV_HEADS)[None, :], BF16)
            q_bd = jnp.einsum('hdbt,hn->bhtnd', qs, onehot).reshape(bs, FOX_HEADS * ts, FOX_KV_HEADS * FOX_HD)
            cum_s = cum[n_p:, :FOX_HEADS].reshape(bs, ts, FOX_HEADS)
            cq = jnp.swapaxes(cum_s, 1, 2).reshape(bs, FOX_HEADS * ts, 1)
            neg_cq = jnp.broadcast_to(-jnp.swapaxes(cum_s, 1, 2)[:, :, None, :], (bs, FOX_HEADS, ts, ts))
            neg_cq = _pad_axis(neg_cq.reshape(bs, FOX_HEADS * ts, ts), 2, LANES)
            expand = jnp.asarray(np.repeat(np.eye(FOX_HEADS), ts, axis=0), BF16)
            k_new_t = _pad_axis(jnp.swapaxes(kf[n_p:].reshape(bs, ts, -1), 1, 2), 2, PAGE_SIZE)
            v_new_t = _pad_axis(jnp.swapaxes(vf[n_p:].reshape(bs, ts, -1), 1, 2), 2, PAGE_SIZE)
            att_s = _fox_paged(page_table, q_bd, cq, neg_cq, expand, k_new_t, v_new_t, fk_pool_t, fv_pool_t, lf_pool_t, o, ts)
            att_s = jnp.transpose(att_s.reshape(bs, FOX_HEADS, ts, FOX_HD), (0, 2, 1, 3)).reshape(n_s, FOX_HEADS * FOX_HD)
            x = _outproj_residual(x, ng[N_MIX_POST], [att_s], [w_out_c[o]], n_p)
            lf = logf[:, :FOX_HEADS]
            outs["fk_p"].append(kf[:n_p].reshape(bp, tp, FOX_KV_HEADS, FOX_HD))
            outs["fv_p"].append(vf[:n_p].reshape(bp, tp, FOX_KV_HEADS, FOX_HD))
            outs["flf_p"].append(lf[:n_p].reshape(bp, tp, FOX_HEADS))
            outs["fk_s"].append(kf[n_p:].reshape(bs, ts, FOX_KV_HEADS, FOX_HD))
            outs["fv_s"].append(vf[n_p:].reshape(bs, ts, FOX_KV_HEADS, FOX_HD))
            outs["flf_s"].append(lf[n_p:].reshape(bs, ts, FOX_HEADS))
        mk, mv = _norm_matmul(mem_flat, ng[N_MEM_SRC], [w_mem_k[l], w_mem_v[l]], [1.0, 1.0], [F32, F32])
        mk, mv = mk.reshape(bp, m_tok, mem_w), mv.reshape(bp, m_tok, mem_w)
        (mq,) = _norm_matmul(x, ng[N_MEM_PRE], [w_mem_q[l]], [float(MEM_HD ** -0.5)], [F32])
        om_p = _mem_attn(mq[:n_p].reshape(bp, tp, mem_w), mk, mv, 1, _pick_tile(tp, TOKEN_TILE), BF16)
        x = _outproj_residual(x, ng[N_MEM_POST], [om_p.reshape(n_p, mem_w)], [w_mem_o[l]], 0)
        grp = _pick_tile(bs, MEM_GROUP) if bs % 8 == 0 else 1
        mq_t = jnp.transpose(sample_rows(mq[n_p:], mem_w).reshape(bs, 8, MEM_HEADS, MEM_HD), (0, 3, 2, 1))
        om_s = _mem_attn_sample(mq_t.reshape(bs, MEM_HD, MEM_HEADS * 8).astype(BF16),
                                cache_mem_k[l].reshape(bs, m_tok * MEM_HEADS, MEM_HD),
                                cache_mem_v[l].reshape(bs, m_tok * MEM_HEADS, MEM_HD), grp)
        om_s = jnp.transpose(om_s.reshape(bs, MEM_HEADS, 8, MEM_HD)[:, :, :ts], (0, 2, 1, 3)).reshape(n_s, mem_w)
        x = _outproj_residual(x, ng[N_MEM_POST], [om_s], [w_mem_o[l]], n_p)
        outs["mk_p"].append(mk.reshape(bp, m_tok, MEM_HEADS, MEM_HD))
        outs["mv_p"].append(mv.reshape(bp, m_tok, MEM_HEADS, MEM_HD))
        x = _ffn_half(x, ng[N_FFN2_PRE], ng[N_FFN2_POST], w_ffn_gate[l, 1], w_ffn_up[l, 1], w_ffn_down[l, 1])

    st = lambda k: jnp.stack(outs[k])
    return (x[:n_p].reshape(bp, tp, d), x[n_p:].reshape(bs, ts, d),
            st("lat_p"), st("kr_p"), st("hg_p"), st("fk_p"), st("fv_p"), st("flf_p"), st("mk_p"), st("mv_p"),
            st("lat_s"), st("kr_s"), st("hg_s"), st("fk_s"), st("fv_s"), st("flf_s"))
```

```python
import functools
import math

import jax
import jax.numpy as jnp
import numpy as np
from jax import lax
from jax.experimental import pallas as pl
from jax.experimental.pallas import tpu as pltpu

F32 = jnp.float32
BF16 = jnp.bfloat16

MLA_HEADS = 8
MLA_NOPE = 64
MLA_ROPE = 32
MLA_V = 64
MLA_KV_LORA = 256
MLA_Q_LORA = 384
ROPE_THETA = 10000.0
HG_HEADS = 4
HG_DK = 128
HG_DV = 128
FOX_HEADS = 16
FOX_KV_HEADS = 4
FOX_HD = 64
MEM_HEADS = 4
MEM_HD = 128
PAGE_SIZE = 128
EPS = 1e-6
LOG2E = math.log2(math.e)
N_FFN1_PRE, N_FFN1_POST, N_MIX_PRE, N_MIX_POST, N_MEM_PRE, N_MEM_POST, N_FFN2_PRE, N_FFN2_POST, N_MEM_SRC = range(9)

LANES = 128
VMEM_LIMIT = 56 * 1024 * 1024
TOKEN_TILE = 512
INPROJ_TILE = 256
FFN_TILE = 768
FFN_CHUNK = 512
ATTN_BLOCK = 512
HGRN_BLOCK = 256
HGRN_CHUNK = 16
MLA_FLASH_HEADS = 8
FOX_FLASH_HEADS = 16
PAGE_GROUPS = 2
MLA_PAGES_PER_STEP = 64
PAGES_PER_STEP = 32
MEM_GROUP = 8


def _params(*sem):
    return pltpu.CompilerParams(dimension_semantics=sem, vmem_limit_bytes=VMEM_LIMIT)


def _rms(x, g):
    return x * lax.rsqrt(jnp.mean(x * x, axis=-1, keepdims=True) + EPS) * g


def _dot(a, b):
    return jnp.dot(a.astype(BF16), b.astype(BF16), preferred_element_type=F32)


def _dot_nt(a, b):
    return lax.dot_general(a.astype(BF16), b.astype(BF16), (((1,), (1,)), ((), ())), preferred_element_type=F32)


def _dot_tn(a, b):
    return lax.dot_general(a.astype(BF16), b.astype(BF16), (((0,), (0,)), ((), ())), preferred_element_type=F32)


def _split3(a):
    hi = a.astype(BF16)
    r = a - hi.astype(F32)
    mid = r.astype(BF16)
    lo = (r - mid.astype(F32)).astype(BF16)
    return hi, mid, lo


def _dot_exact_rhs(m01, a):
    hi, mid, lo = _split3(a)
    return (jnp.dot(m01, hi, preferred_element_type=F32) + jnp.dot(m01, mid, preferred_element_type=F32)
            + jnp.dot(m01, lo, preferred_element_type=F32))


def _dot_exact_lhs(a, m01):
    hi, mid, lo = _split3(a)
    return (jnp.dot(hi, m01, preferred_element_type=F32) + jnp.dot(mid, m01, preferred_element_type=F32)
            + jnp.dot(lo, m01, preferred_element_type=F32))


def _log_sigmoid(x):
    return jnp.minimum(x, 0.0) - jnp.log(1.0 + jnp.exp(-jnp.abs(x)))


def _pick_tile(n, target):
    t = min(n, target)
    while n % t:
        t -= 8
    assert t > 0 and t % 8 == 0, (n, target)
    return t


def _full(shape):
    return pl.BlockSpec(shape, lambda *_: (0,) * len(shape))


def _resident(shape):
    return pl.BlockSpec(shape, lambda *_: (0,) * len(shape), pipeline_mode=pl.Buffered(1))


def _ffn_kernel(x_ref, gpre_ref, gpost_ref, wg_ref, wu_ref, wd_ref, o_ref, acc_ref, *, chunks):
    x = x_ref[...]
    h = _rms(x, gpre_ref[...]).astype(BF16)
    for idx, (c0, cw) in enumerate(chunks):
        g = jnp.dot(h, wg_ref[:, c0:c0 + cw], preferred_element_type=F32)
        u = jnp.dot(h, wu_ref[:, c0:c0 + cw], preferred_element_type=F32)
        a = (g * jax.nn.sigmoid(g) * u).astype(BF16)
        d = jnp.dot(a, wd_ref[c0:c0 + cw, :], preferred_element_type=F32)
        if idx == 0:
            acc_ref[...] = d
        else:
            acc_ref[...] += d
    o_ref[...] = x + 0.5 * _rms(acc_ref[...], gpost_ref[...])


def _ffn_half(x, g_pre, g_post, w_gate, w_up, w_down):
    n, d = x.shape
    f = w_gate.shape[1]
    tm = _pick_tile(n, FFN_TILE)
    chunks = tuple((c0, min(FFN_CHUNK, f - c0)) for c0 in range(0, f, FFN_CHUNK))
    return pl.pallas_call(
        functools.partial(_ffn_kernel, chunks=chunks),
        grid=(n // tm,),
        in_specs=[pl.BlockSpec((tm, d), lambda i: (i, 0)), _full((1, d)), _full((1, d)),
                  _resident((d, f)), _resident((d, f)), _resident((f, d))],
        out_specs=pl.BlockSpec((tm, d), lambda i: (i, 0)),
        out_shape=jax.ShapeDtypeStruct((n, d), F32),
        scratch_shapes=[pltpu.VMEM((tm, d), F32)],
        compiler_params=_params("parallel"),
        name="ffn_half",
    )(x, g_pre.reshape(1, d), g_post.reshape(1, d), w_gate.astype(BF16), w_up.astype(BF16), w_down.astype(BF16))


def _norm_matmul_kernel(x_ref, g_ref, *refs, scales):
    n_out = len(scales)
    h = _rms(x_ref[...], g_ref[...]).astype(BF16)
    for w_ref, o_ref, s in zip(refs[:n_out], refs[n_out:], scales):
        y = jnp.dot(h, w_ref[...], preferred_element_type=F32)
        o_ref[...] = (y if s == 1.0 else y * s).astype(o_ref.dtype)


def _norm_matmul(x, g, ws, scales, dtypes):
    n, d = x.shape
    tm = _pick_tile(n, TOKEN_TILE)
    return pl.pallas_call(
        functools.partial(_norm_matmul_kernel, scales=tuple(scales)),
        grid=(n // tm,),
        in_specs=[pl.BlockSpec((tm, d), lambda i: (i, 0)), _full((1, d))] + [_resident(w.shape) for w in ws],
        out_specs=[pl.BlockSpec((tm, w.shape[1]), lambda i: (i, 0)) for w in ws],
        out_shape=[jax.ShapeDtypeStruct((n, w.shape[1]), dt) for w, dt in zip(ws, dtypes)],
        compiler_params=_params("parallel"),
        name="norm_matmul",
    )(x, g.reshape(1, d), *[w.astype(BF16) for w in ws])


def _outproj_kernel(x_ref, g_ref, *refs, n_parts):
    acc = None
    for a_ref, w_ref in zip(refs[:n_parts], refs[n_parts:2 * n_parts]):
        y = jnp.dot(a_ref[...].astype(BF16), w_ref[...], preferred_element_type=F32)
        acc = y if acc is None else acc + y
    refs[2 * n_parts][...] = x_ref[...] + _rms(acc, g_ref[...])


def _outproj_residual(x, g, parts, ws, row0):
    n, d = x.shape
    rows = parts[0].shape[0]
    tm = _pick_tile(rows, TOKEN_TILE)
    assert row0 % tm == 0
    off = row0 // tm
    return pl.pallas_call(
        functools.partial(_outproj_kernel, n_parts=len(parts)),
        grid=(rows // tm,),
        in_specs=[pl.BlockSpec((tm, d), lambda i: (i + off, 0)), _full((1, d))]
        + [pl.BlockSpec((tm, p.shape[1]), lambda i: (i, 0)) for p in parts] + [_resident(w.shape) for w in ws],
        out_specs=pl.BlockSpec((tm, d), lambda i: (i + off, 0)),
        out_shape=jax.ShapeDtypeStruct((n, d), F32),
        input_output_aliases={0: 0},
        compiler_params=_params("parallel"),
        name="outproj_residual",
    )(x, g.reshape(1, d), *parts, *[w.astype(BF16) for w in ws])


def _even_inproj_kernel(x_ref, gpre_ref, cos_ref, sin_ref, cost_ref, sint_ref, wq_ref, wc_ref, wkr_ref, wh_ref, wuqt_ref,
                        wuk_ref, wuvt_ref, gq_ref, gkv_ref, lb_ref,
                        qt_ref, kp_ref, vt_ref, c_ref, krp_ref, hq_ref, logf_ref, kk_ref, hi_ref, gate_ref, *, q_scale):
    h = _rms(x_ref[...], gpre_ref[...]).astype(BF16)
    cos_t = cos_ref[...]
    sin_t = sin_ref[...]
    cos_tt = cost_ref[...]
    sin_tt = sint_ref[...]
    nh = kp_ref.shape[0]
    qn = _rms(jnp.dot(h, wq_ref[...], preferred_element_type=F32), gq_ref[...]).astype(BF16)
    q2t = _dot_nt(wuqt_ref[...], qn)
    for i in range(nh):
        a = q2t[i * LANES:(i + 1) * LANES, :]
        b = q2t[(nh + i) * LANES:(nh + i + 1) * LANES, :]
        qt_ref[i * LANES:(i + 1) * LANES, :] = ((a * cos_tt + b * sin_tt) * q_scale).astype(qt_ref.dtype)
    c = _rms(jnp.dot(h, wc_ref[...], preferred_element_type=F32), gkv_ref[...])
    c_ref[...] = c
    kr2 = jnp.dot(h, wkr_ref[...], preferred_element_type=F32)
    krp = kr2[:, :LANES] * cos_t + kr2[:, LANES:] * sin_t
    krp_ref[...] = krp
    cb = c.astype(BF16)
    kn = jnp.dot(cb, wuk_ref[...], preferred_element_type=F32)
    for i in range(nh):
        kp_ref[i] = (kn[:, i * LANES:(i + 1) * LANES] + krp).astype(kp_ref.dtype)
    vt_ref[...] = _with_ones_rows(_dot_nt(wuvt_ref[...], cb)).astype(vt_ref.dtype)
    hh = jnp.dot(h, wh_ref[...], preferred_element_type=F32)
    w = hq_ref.shape[1]
    lb = lb_ref[...]
    z = hh[:, w:2 * w]
    hg = hh[:, 3 * w:]
    hq_ref[...] = hh[:, :w]
    logf_ref[...] = jnp.log(lb + (1.0 - lb) * jax.nn.sigmoid(z))
    kk_ref[...] = (1.0 - lb) * jax.nn.sigmoid(-z)
    hi_ref[...] = hh[:, 2 * w:3 * w]
    gate_ref[...] = hg * jax.nn.sigmoid(hg)


def _with_ones_rows(vt):
    pos = lax.broadcasted_iota(jnp.int32, vt.shape, 0) & (2 * LANES - 1)
    return jnp.where((pos == LANES - 1) | (pos == LANES), 1.0, vt)


def _rot_cols(w):
    half = w.shape[-1] // 2
    return jnp.concatenate([-w[..., half:], w[..., :half]], axis=-1)


def _even_inproj(x, g_pre, cos_t, sin_t, cos_tt, sin_tt, w_in, g_q, g_kv, w_uq, w_uk, w_uv, lb, n_prompt):
    n, d = x.shape
    tm = _pick_tile(math.gcd(n_prompt, n - n_prompt), INPROJ_TILE)
    nh, hw = MLA_HEADS, HG_HEADS * HG_DK
    o_q, o_c, o_kr = MLA_Q_LORA, MLA_Q_LORA + MLA_KV_LORA, MLA_Q_LORA + MLA_KV_LORA + MLA_ROPE
    wq, wc, wkr, wh = w_in[:, :o_q], w_in[:, o_q:o_c], w_in[:, o_c:o_kr], w_in[:, o_kr:]
    wkr2 = jnp.zeros((d, 2 * LANES), F32)
    wkr2 = wkr2.at[:, MLA_NOPE:MLA_NOPE + MLA_ROPE].set(wkr).at[:, LANES + MLA_NOPE:LANES + MLA_NOPE + MLA_ROPE].set(_rot_cols(wkr))
    uq = w_uq.reshape(MLA_Q_LORA, nh, MLA_NOPE + MLA_ROPE)
    wuq2 = jnp.zeros((MLA_Q_LORA, 2, nh, LANES), F32)
    wuq2 = wuq2.at[:, 0, :, :MLA_NOPE + MLA_ROPE].set(uq)
    wuq2 = wuq2.at[:, 1, :, MLA_NOPE:MLA_NOPE + MLA_ROPE].set(_rot_cols(uq[..., MLA_NOPE:]))
    wuq2 = wuq2.reshape(MLA_Q_LORA, 2 * nh * LANES)
    wuk = jnp.zeros((MLA_KV_LORA, nh, LANES), F32).at[:, :, :MLA_NOPE].set(w_uk).reshape(MLA_KV_LORA, nh * LANES)
    wuv = jnp.zeros((MLA_KV_LORA, nh // 2, 2, 2, MLA_V), F32)
    uv = w_uv.reshape(MLA_KV_LORA, nh // 2, 2, MLA_V)
    wuv = wuv.at[:, :, 0, 0].set(uv[:, :, 0]).at[:, :, 1, 1].set(uv[:, :, 1]).reshape(MLA_KV_LORA, nh * LANES)
    ws = [w.astype(BF16) for w in (wq, wc, wkr2, wh, wuq2.T, wuk, wuv.T)]
    tok = lambda w: pl.BlockSpec((tm, w), lambda i: (i, 0))
    tok_t = lambda r: pl.BlockSpec((r, tm), lambda i: (0, i))
    head = pl.BlockSpec((nh, tm, LANES), lambda i: (0, i, 0))
    outs = pl.pallas_call(
        functools.partial(_even_inproj_kernel, q_scale=float((MLA_NOPE + MLA_ROPE) ** -0.5) * LOG2E),
        grid=(n // tm,),
        in_specs=[tok(d), _full((1, d)), tok(LANES), tok(LANES), tok_t(LANES), tok_t(LANES)]
        + [_resident(w.shape) for w in ws] + [_full((1, MLA_Q_LORA)), _full((1, MLA_KV_LORA)), _full((1, hw))],
        out_specs=[tok_t(nh * LANES), head, tok_t(nh * LANES), tok(MLA_KV_LORA), tok(LANES)] + [tok(hw)] * 5,
        out_shape=[jax.ShapeDtypeStruct((nh * LANES, n), BF16), jax.ShapeDtypeStruct((nh, n, LANES), BF16),
                   jax.ShapeDtypeStruct((nh * LANES, n), BF16),
                   jax.ShapeDtypeStruct((n, MLA_KV_LORA), F32), jax.ShapeDtypeStruct((n, LANES), F32)]
        + [jax.ShapeDtypeStruct((n, hw), F32)] * 5,
        compiler_params=_params("parallel"),
        name="even_inproj",
    )(x, g_pre.reshape(1, d), cos_t, sin_t, cos_tt, sin_tt, *ws, g_q.reshape(1, -1), g_kv.reshape(1, -1),
      lb.reshape(1, -1))
    return outs


def _odd_inproj_kernel(x_ref, gpre_ref, wqt_ref, wk_ref, wv2t_ref, wkf_ref, wvf_ref, wf_ref, bf_ref,
                       selqt_ref, selk_ref,
                       qat_ref, ka_ref, v2t_ref, kf_ref, vf_ref, logf_ref, cum_ref, carry_ref,
                       *, n_prompt_tiles, tiles_per_seq, sample_shift):
    i = pl.program_id(0)
    tm = x_ref.shape[0]
    is_sample = i >= n_prompt_tiles

    @pl.when(i == 0)
    def _():
        carry_ref[...] = jnp.zeros(carry_ref.shape, F32)

    h = _rms(x_ref[...], gpre_ref[...]).astype(BF16)
    lane = lax.broadcasted_iota(jnp.int32, (tm, LANES), 1)
    fz = jnp.dot(h, wf_ref[...], preferred_element_type=F32) + bf_ref[...]
    logf = jnp.where(lane < FOX_HEADS, _log_sigmoid(fz), 0.0)
    logf_ref[...] = logf
    row = lax.broadcasted_iota(jnp.int32, (tm, tm), 0)
    col = lax.broadcasted_iota(jnp.int32, (tm, tm), 1)
    shift = jnp.where(is_sample, sample_shift, 30)
    tri = jnp.where((col <= row) & ((row >> shift) == (col >> shift)), 1.0, 0.0).astype(BF16)
    fresh = jnp.logical_or(is_sample, i % tiles_per_seq == 0)
    carry = jnp.where(fresh, 0.0, carry_ref[...])
    cum = _dot_exact_rhs(tri, logf) + carry
    cum_ref[...] = cum
    carry_ref[...] = cum[tm - 1:tm, :]
    hi, mid, lo = _split3(cum * LOG2E)
    hi = jnp.where(lane == FOX_HEADS, 1.0, hi)
    qat_ref[...] = (_dot_nt(wqt_ref[...], h) + _dot_nt(selqt_ref[0], hi) + _dot_nt(selqt_ref[1], mid)
                    + _dot_nt(selqt_ref[2], lo)).astype(qat_ref.dtype)
    ka = (jnp.dot(h, wk_ref[...], preferred_element_type=F32) + jnp.dot(hi, selk_ref[0], preferred_element_type=F32)
          + jnp.dot(mid, selk_ref[1], preferred_element_type=F32) + jnp.dot(lo, selk_ref[2], preferred_element_type=F32))
    for j in range(ka_ref.shape[0]):
        ka_ref[j] = ka[:, j * LANES:(j + 1) * LANES].astype(ka_ref.dtype)
    v2t_ref[...] = _with_ones_rows(_dot_nt(wv2t_ref[...], h)).astype(v2t_ref.dtype)
    kf_ref[...] = jnp.dot(h, wkf_ref[...], preferred_element_type=F32)
    vf_ref[...] = jnp.dot(h, wvf_ref[...], preferred_element_type=F32)


def _fox_selectors():
    nq, nk, g = FOX_HEADS, FOX_KV_HEADS, FOX_HEADS // FOX_KV_HEADS
    selq = np.zeros((3, LANES, nq * LANES), np.float32)
    selk = np.zeros((3, LANES, nk * LANES), np.float32)
    base = FOX_HD
    ones_row = FOX_HEADS
    for hd in range(nq):
        n, gi = divmod(hd, g)
        for t in range(3):
            selq[t, hd, hd * LANES + base + t] = 1.0
            selq[0, ones_row, hd * LANES + base + 3 + 3 * gi + t] = 1.0
            selk[t, hd, n * LANES + base + 3 + 3 * gi + t] = -1.0
    for n in range(nk):
        selk[0, ones_row, n * LANES + base:n * LANES + base + 3] = 1.0
    return jnp.asarray(np.swapaxes(selq, 1, 2), BF16), jnp.asarray(selk, BF16)


def _odd_inproj(x, g_pre, w_in, b_f, n_prompt, seq_prompt, seq_sample):
    n, d = x.shape
    tm = _pick_tile(math.gcd(math.gcd(n_prompt, n - n_prompt), seq_prompt), INPROJ_TILE)
    assert n_prompt % tm == 0 and seq_prompt % tm == 0 and tm % seq_sample == 0
    assert seq_sample & (seq_sample - 1) == 0, "sample sequences must be a power of two long"
    nq, nk, hd = FOX_HEADS, FOX_KV_HEADS, FOX_HD
    o_q, o_k, o_v = nq * hd, nq * hd + nk * hd, nq * hd + 2 * nk * hd
    w_q, w_k, w_v, w_f = w_in[:, :o_q], w_in[:, o_q:o_k], w_in[:, o_k:o_v], w_in[:, o_v:]
    wq = jnp.zeros((d, nq, LANES), F32).at[:, :, :hd].set(w_q.reshape(d, nq, hd) * (float(hd ** -0.5) * LOG2E)).reshape(d, nq * LANES)
    wk = jnp.zeros((d, nk, LANES), F32).at[:, :, :hd].set(w_k.reshape(d, nk, hd)).reshape(d, nk * LANES)
    wv2 = jnp.zeros((d, nk, 2, 2, hd), F32)
    wv2 = wv2.at[:, :, 0, 0].set(w_v.reshape(d, nk, hd)).at[:, :, 1, 1].set(w_v.reshape(d, nk, hd)).reshape(d, 2 * nk * LANES)
    wf = jnp.zeros((d, LANES), F32).at[:, :nq].set(w_f)
    bf = jnp.zeros((1, LANES), F32).at[0, :nq].set(b_f)
    selq_t, selk = _fox_selectors()
    ws = [w.astype(BF16) for w in (wq.T, wk, wv2.T, w_k, w_v, wf)]
    tok = lambda w: pl.BlockSpec((tm, w), lambda i: (i, 0))
    tok_t = lambda r: pl.BlockSpec((r, tm), lambda i: (0, i))
    head = lambda k: pl.BlockSpec((k, tm, LANES), lambda i: (0, i, 0))
    return pl.pallas_call(
        functools.partial(_odd_inproj_kernel, n_prompt_tiles=n_prompt // tm, tiles_per_seq=seq_prompt // tm,
                          sample_shift=int(seq_sample).bit_length() - 1),
        grid=(n // tm,),
        in_specs=[tok(d), _full((1, d))] + [_resident(w.shape) for w in ws]
        + [_full((1, LANES)), _resident(selq_t.shape), _resident(selk.shape)],
        out_specs=[tok_t(nq * LANES), head(nk), tok_t(2 * nk * LANES), tok(nk * hd), tok(nk * hd), tok(LANES), tok(LANES)],
        out_shape=[jax.ShapeDtypeStruct((nq * LANES, n), BF16), jax.ShapeDtypeStruct((nk, n, LANES), BF16),
                   jax.ShapeDtypeStruct((2 * nk * LANES, n), BF16), jax.ShapeDtypeStruct((n, nk * hd), F32),
                   jax.ShapeDtypeStruct((n, nk * hd), F32), jax.ShapeDtypeStruct((n, LANES), F32),
                   jax.ShapeDtypeStruct((n, LANES), F32)],
        scratch_shapes=[pltpu.VMEM((1, LANES), F32)],
        compiler_params=_params("arbitrary"),
        name="odd_inproj",
    )(x, g_pre.reshape(1, d), *ws, bf, selq_t, selk)


def _flash_kernel(qi_ref, kj_ref, qt_ref, k_ref, vt_ref, o_ref, m_ref, acc_ref, *, hps, hpk):
    p = pl.program_id(2)
    i = qi_ref[p]
    j = kj_ref[p]
    tk, tq = k_ref.shape[1], qt_ref.shape[1]

    @pl.when(j == 0)
    def _():
        m_ref[...] = jnp.full(m_ref.shape, -jnp.inf, F32)
        acc_ref[...] = jnp.zeros(acc_ref.shape, F32)

    def update(masked):
        if masked:
            keep = lax.broadcasted_iota(jnp.int32, (tk, tq), 0) <= lax.broadcasted_iota(jnp.int32, (tk, tq), 1)
        def scores(hh):
            return jnp.dot(k_ref[hh // hpk], qt_ref[hh * LANES:(hh + 1) * LANES, :],
                           preferred_element_type=F32)

        def accumulate(hh, alpha, pb):
            vi = hh if hpk == 1 else (hh // hpk) * 2 + hh % 2
            vt = vt_ref[vi * LANES:(vi + 1) * LANES, :]
            acc_ref[hh] = alpha * acc_ref[hh] + jnp.dot(vt, pb, preferred_element_type=F32)

        s_next = scores(0)
        pending = None
        for hh in range(hps):
            s = s_next
            if hh + 1 < hps:
                s_next = scores(hh + 1)
            if pending is not None:
                accumulate(*pending)
            if masked:
                s = jnp.where(keep, s, -jnp.inf)
            m_prev = m_ref[hh]
            m_new = jnp.maximum(m_prev, jnp.max(s, axis=0, keepdims=True))
            alpha = jnp.exp2(m_prev - m_new)
            pending = (hh, alpha, jnp.exp2(s - m_new).astype(BF16))
            m_ref[hh] = m_new
        accumulate(*pending)

    @pl.when(j < i)
    def _():
        update(False)

    @pl.when(j == i)
    def _():
        update(True)
        top = lax.broadcasted_iota(jnp.int32, (LANES, tq), 0) < LANES // 2
        for r in range(hps // 2):
            a_e, a_o = acc_ref[2 * r], acc_ref[2 * r + 1]
            ot = jnp.where(top, a_e / a_e[LANES - 1:LANES, :], a_o / a_o[0:1, :])
            o_ref[:, r * LANES:(r + 1) * LANES] = ot.T.astype(o_ref.dtype)


def _flash_causal(qt, k, vt, n_seq, seq_len, hps, hpk):
    nh = qt.shape[0] // LANES
    groups = nh // hps
    kh = hps // hpk
    vrows = (hps if hpk == 1 else 2 * kh) * LANES
    blk = _pick_tile(seq_len, ATTN_BLOCK)
    nb = seq_len // blk
    pairs = [(a, b) for a in range(nb) for b in range(a + 1)]
    qi = jnp.asarray([a for a, _ in pairs], jnp.int32)
    kj = jnp.asarray([b for _, b in pairs], jnp.int32)
    gs = pltpu.PrefetchScalarGridSpec(
        num_scalar_prefetch=2, grid=(n_seq, groups, len(pairs)),
        in_specs=[pl.BlockSpec((hps * LANES, blk), lambda b, g, p, qi, kj: (g, b * nb + qi[p])),
                  pl.BlockSpec((kh, blk, LANES), lambda b, g, p, qi, kj: (g, b * nb + kj[p], 0)),
                  pl.BlockSpec((vrows, blk), lambda b, g, p, qi, kj: (g, b * nb + kj[p]))],
        out_specs=pl.BlockSpec((blk, hps * LANES // 2), lambda b, g, p, qi, kj: (b * nb + qi[p], g)),
        scratch_shapes=[pltpu.VMEM((hps, 1, blk), F32), pltpu.VMEM((hps, LANES, blk), F32)])
    return pl.pallas_call(
        functools.partial(_flash_kernel, hps=hps, hpk=hpk),
        grid_spec=gs,
        out_shape=jax.ShapeDtypeStruct((n_seq * seq_len, nh * LANES // 2), BF16),
        compiler_params=_params("parallel", "parallel", "arbitrary"),
        name="flash_causal",
    )(qi, kj, qt, k, vt)


def _hgrn_kernel(hq_ref, kk_ref, hi_ref, logf_ref, gate_ref, s0_ref, gout_ref, o_ref, s_ref, obuf_ref, *, chunk):
    j = pl.program_id(1)
    tb = hq_ref.shape[0]
    nh = s_ref.shape[1]
    dk = s_ref.shape[2]

    @pl.when(j == 0)
    def _():
        s_ref[...] = s0_ref[...]

    row = lax.broadcasted_iota(jnp.int32, (chunk, chunk), 0)
    col = lax.broadcasted_iota(jnp.int32, (chunk, chunk), 1)
    causal = col <= row
    tri = jnp.where(causal, 1.0, 0.0).astype(BF16)

    def step(c, carry):
        r0 = pl.multiple_of(c * chunk, chunk)
        heads = range(nh)
        sls = [(pl.ds(r0, chunk), slice(hh * dk, (hh + 1) * dk)) for hh in heads]
        qs = [hq_ref[sl] for sl in sls]
        ks = [kk_ref[sl] for sl in sls]
        vs = [hi_ref[sl] for sl in sls]
        cums = [_dot_exact_rhs(tri, logf_ref[sl]) for sl in sls]
        amats = [jnp.zeros((chunk, chunk), F32) for _ in heads]
        for s in range(chunk):
            for hh in heads:
                w = qs[hh] * ks[hh][s:s + 1, :] * jnp.exp(jnp.minimum(cums[hh] - cums[hh][s:s + 1, :], 0.0))
                amats[hh] = jnp.where(col == s, jnp.sum(w, axis=-1, keepdims=True), amats[hh])
        states = [s_ref[0, hh] for hh in heads]
        lasts = [cums[hh][chunk - 1:chunk, :] for hh in heads]
        outs = [_dot(jnp.where(causal, amats[hh], 0.0), vs[hh]) + _dot(qs[hh] * jnp.exp(cums[hh]), states[hh])
                for hh in heads]
        upds = [_dot_tn(ks[hh] * jnp.exp(lasts[hh] - cums[hh]), vs[hh]) for hh in heads]
        decays = [jnp.broadcast_to(jnp.exp(lasts[hh]), (8, dk)).T[:, 0:1] for hh in heads]
        for hh in heads:
            obuf_ref[sls[hh]] = outs[hh]
            s_ref[0, hh] = decays[hh] * states[hh] + upds[hh]
        return carry

    lax.fori_loop(0, tb // chunk, step, 0)
    for hh in range(nh):
        sl = (slice(None), slice(hh * dk, (hh + 1) * dk))
        o_ref[sl] = (_rms(obuf_ref[sl], gout_ref[sl]) * gate_ref[sl]).astype(o_ref.dtype)


def _hgrn(hq, kk, hi, logf, gate, s0, g_out, n_seq, seq_len, block, chunk, dtype):
    w = hq.shape[1]
    nb = seq_len // block
    tok = pl.BlockSpec((block, w), lambda s, j: (s * nb + j, 0))
    st = pl.BlockSpec((1,) + s0.shape[1:], lambda s, j: (s, 0, 0, 0))
    return pl.pallas_call(
        functools.partial(_hgrn_kernel, chunk=chunk),
        grid=(n_seq, nb),
        in_specs=[tok] * 5 + [st, _full((1, w))],
        out_specs=[tok, st],
        out_shape=[jax.ShapeDtypeStruct((n_seq * seq_len, w), dtype), jax.ShapeDtypeStruct(s0.shape, F32)],
        scratch_shapes=[pltpu.VMEM((block, w), F32)],
        compiler_params=_params("parallel", "arbitrary"),
        name="hgrn2",
    )(hq, kk, hi, logf, gate, s0, g_out.reshape(1, w))


def _headmm_kernel(x_ref, w_ref, o_ref):
    o_ref[...] = jnp.dot(x_ref[...].astype(BF16), w_ref[...], preferred_element_type=F32).astype(o_ref.dtype)


def _headmm(x, w, rows, row0, dtype):
    nh, _, kdim = x.shape
    assert row0 % rows == 0
    off = row0 // rows
    return pl.pallas_call(
        _headmm_kernel,
        grid=(nh,),
        in_specs=[pl.BlockSpec((None, rows, kdim), lambda h: (h, off, 0)),
                  pl.BlockSpec((None,) + w.shape[1:], lambda h: (h, 0, 0))],
        out_specs=pl.BlockSpec((None, rows, w.shape[2]), lambda h: (h, 0, 0)),
        out_shape=jax.ShapeDtypeStruct((nh, rows, w.shape[2]), dtype),
        compiler_params=_params("parallel"),
        name="headmm",
    )(x, w.astype(BF16))


def _online_update(m_ref, l_ref, acc_ref, s, pv):
    m_prev = m_ref[...]
    m_new = jnp.maximum(m_prev, jnp.max(s, axis=-1, keepdims=True))
    alpha = jnp.exp2(m_prev - m_new)
    pe = jnp.exp2(s - m_new)
    l_ref[...] = alpha * l_ref[...] + jnp.sum(pe, axis=-1, keepdims=True)
    acc_ref[...] = alpha * acc_ref[...] + pv(pe.astype(BF16))
    m_ref[...] = m_new


def _new_token_mask(rows, t_new):
    r = lax.broadcasted_iota(jnp.int32, (rows, LANES), 0)
    c = lax.broadcasted_iota(jnp.int32, (rows, LANES), 1)
    return (c < t_new) & (c <= (r & (t_new - 1)))


def _page_copies(pt_ref, pools, bufs, sems, b, j, slot, *, layer, pps, n_pages, latest_first):
    out = []
    for i in range(pps):
        idx = j * pps + i
        pg = pt_ref[b, (n_pages - 1 - idx) if latest_first else idx]
        for k, (pool, buf) in enumerate(zip(pools, bufs)):
            out.append(pltpu.make_async_copy(pool.at[layer, pg], buf.at[slot, i], sems.at[k, slot]))
    return out


def _gather_pages(pt_ref, pools, bufs, sems, **kw):
    b, j = pl.program_id(0), pl.program_id(1)
    nb, nj = pl.num_programs(0), pl.num_programs(1)
    t = b * nj + j
    slot = lax.rem(t, 2)
    copies = functools.partial(_page_copies, pt_ref, pools, bufs, sems, **kw)

    @pl.when(t == 0)
    def _():
        for c in copies(b, j, slot):
            c.start()

    @pl.when(t + 1 < nb * nj)
    def _():
        t1 = t + 1
        for c in copies(lax.div(t1, nj), lax.rem(t1, nj), 1 - slot):
            c.start()

    for c in copies(b, j, slot):
        c.wait()
    return slot


def _mla_paged_kernel(pt_ref, qa_ref, qr_ref, cnew_ref, krnew_ref, lat_hbm, kr_hbm, o_ref,
                      ck_buf, kr_buf, sems, m_ref, l_ref, acc_ref, *, pps, t_new, layer, n_pages):
    j = pl.program_id(1)
    rows = qa_ref.shape[0]
    slot = _gather_pages(pt_ref, (lat_hbm, kr_hbm), (ck_buf, kr_buf), sems, layer=layer, pps=pps, n_pages=n_pages,
                         latest_first=False)

    @pl.when(j == 0)
    def _():
        m_ref[...] = jnp.full(m_ref.shape, -jnp.inf, F32)
        l_ref[...] = jnp.zeros(l_ref.shape, F32)
        acc_ref[...] = jnp.zeros(acc_ref.shape, F32)

    qa = qa_ref[...]
    qr = qr_ref[...]
    cks = [ck_buf[slot, i].astype(BF16) for i in range(pps)]
    groups = [range(g * pps // PAGE_GROUPS, (g + 1) * pps // PAGE_GROUPS) for g in range(PAGE_GROUPS)]
    ss = [jnp.concatenate([_dot_nt(qa, cks[i]) + _dot(qr, kr_buf[slot, i]) for i in grp], axis=-1) for grp in groups]
    for grp, s in zip(groups, ss):
        def pv(p, grp=grp):
            out = None
            for idx, i in enumerate(grp):
                y = jnp.dot(p[:, idx * PAGE_SIZE:(idx + 1) * PAGE_SIZE], cks[i], preferred_element_type=F32)
                out = y if out is None else out + y
            return out

        _online_update(m_ref, l_ref, acc_ref, s, pv)

    @pl.when(j == pl.num_programs(1) - 1)
    def _():
        cn = cnew_ref[...].astype(BF16)
        sn = _dot_nt(qa, cn) + _dot(qr, krnew_ref[...])
        sn = jnp.where(_new_token_mask(rows, t_new), sn, -jnp.inf)
        _online_update(m_ref, l_ref, acc_ref, sn, lambda p: jnp.dot(p, cn, preferred_element_type=F32))
        o_ref[...] = (acc_ref[...] / l_ref[...]).astype(o_ref.dtype)


def _mla_paged(page_table, qa, qr, c_new, kr_new_t, lat_pool, kr_pool_t, layer, t_new):
    bsz, rows, cdim = qa.shape
    n_pages = page_table.shape[1]
    pps = min(MLA_PAGES_PER_STEP, n_pages)
    assert n_pages % pps == 0 and t_new & (t_new - 1) == 0
    rdim = qr.shape[2]
    per_b = lambda shape: pl.BlockSpec((None,) + shape, lambda b, j, pt: (b, 0, 0))
    hbm = pl.BlockSpec(memory_space=pl.ANY)
    gs = pltpu.PrefetchScalarGridSpec(
        num_scalar_prefetch=1, grid=(bsz, n_pages // pps),
        in_specs=[per_b((rows, cdim)), per_b((rows, rdim)), per_b((PAGE_SIZE, cdim)), per_b((rdim, PAGE_SIZE)), hbm, hbm],
        out_specs=per_b((rows, cdim)),
        scratch_shapes=[pltpu.VMEM((2, pps, PAGE_SIZE, cdim), F32), pltpu.VMEM((2, pps, rdim, PAGE_SIZE), F32),
                        pltpu.SemaphoreType.DMA((2, 2)),
                        pltpu.VMEM((rows, 1), F32), pltpu.VMEM((rows, 1), F32), pltpu.VMEM((rows, cdim), F32)])
    return pl.pallas_call(
        functools.partial(_mla_paged_kernel, pps=pps, t_new=t_new, layer=layer, n_pages=n_pages),
        grid_spec=gs,
        out_shape=jax.ShapeDtypeStruct((bsz, rows, cdim), F32),
        compiler_params=_params("arbitrary", "arbitrary"),
        name="mla_paged",
    )(page_table, qa, qr, c_new, kr_new_t, lat_pool, kr_pool_t)


def _fox_paged_kernel(pt_ref, q_ref, cq_ref, ncq_ref, e_ref, knew_ref, vnew_ref, k_hbm, v_hbm, lf_hbm, o_ref,
                      k_buf, v_buf, lf_buf, sems, m_ref, l_ref, acc_ref, suf_ref, *, pps, t_new, n_kv, layer, n_pages):
    j = pl.program_id(1)
    rows = q_ref.shape[0]
    slot = _gather_pages(pt_ref, (k_hbm, v_hbm, lf_hbm), (k_buf, v_buf, lf_buf), sems, layer=layer, pps=pps,
                         n_pages=n_pages, latest_first=True)

    @pl.when(j == 0)
    def _():
        m_ref[...] = jnp.full(m_ref.shape, -jnp.inf, F32)
        l_ref[...] = jnp.zeros(l_ref.shape, F32)
        acc_ref[...] = jnp.zeros(acc_ref.shape, F32)
        suf_ref[...] = jnp.zeros(suf_ref.shape, F32)

    q = q_ref[...]
    cq = cq_ref[...]
    e = e_ref[...]
    r_i = lax.broadcasted_iota(jnp.int32, (PAGE_SIZE, PAGE_SIZE), 0)
    c_i = lax.broadcasted_iota(jnp.int32, (PAGE_SIZE, PAGE_SIZE), 1)
    later = jnp.where(r_i > c_i, 1.0, 0.0).astype(BF16)
    lfs = [_dot_exact_rhs(e, lf_buf[slot, i]) for i in range(pps)]
    sufs = [_dot_exact_lhs(lf, later) for lf in lfs]
    tots = [jnp.sum(lf, axis=-1, keepdims=True) for lf in lfs]
    qks = [_dot(q, k_buf[slot, i]) for i in range(pps)]
    run = suf_ref[...]
    ss = []
    for i in range(pps):
        ss.append(qks[i] + (sufs[i] + (run + cq)) * LOG2E)
        run = run + tots[i]
    suf_ref[...] = run
    for g in range(PAGE_GROUPS):
        grp = range(g * pps // PAGE_GROUPS, (g + 1) * pps // PAGE_GROUPS)

        def pv(p, grp=grp):
            out = None
            for idx, i in enumerate(grp):
                y = _dot_nt(p[:, idx * PAGE_SIZE:(idx + 1) * PAGE_SIZE], v_buf[slot, i])
                out = y if out is None else out + y
            return out

        _online_update(m_ref, l_ref, acc_ref, jnp.concatenate([ss[i] for i in grp], axis=-1), pv)

    @pl.when(j == pl.num_programs(1) - 1)
    def _():
        sn = _dot(q, knew_ref[...]) + (cq + ncq_ref[...]) * LOG2E
        sn = jnp.where(_new_token_mask(rows, t_new), sn, -jnp.inf)
        vn = vnew_ref[...]
        _online_update(m_ref, l_ref, acc_ref, sn, lambda p: _dot_nt(p, vn))
        rk = rows // n_kv
        hd = acc_ref.shape[1] // n_kv
        for n in range(n_kv):
            o_ref[n * rk:(n + 1) * rk, :] = (acc_ref[n * rk:(n + 1) * rk, n * hd:(n + 1) * hd]
                                             / l_ref[n * rk:(n + 1) * rk, :]).astype(o_ref.dtype)


def _fox_paged(page_table, q_bd, cq, neg_cq, expand, k_new_t, v_new_t, k_pool_t, v_pool_t, lf_pool_t, layer, t_new):
    bsz, rows, kd = q_bd.shape
    n_pages = page_table.shape[1]
    nh = lf_pool_t.shape[2]
    pps = min(PAGES_PER_STEP, n_pages)
    assert n_pages % pps == 0 and t_new & (t_new - 1) == 0
    per_b = lambda shape: pl.BlockSpec((None,) + shape, lambda b, j, pt: (b, 0, 0))
    hbm = pl.BlockSpec(memory_space=pl.ANY)
    gs = pltpu.PrefetchScalarGridSpec(
        num_scalar_prefetch=1, grid=(bsz, n_pages // pps),
        in_specs=[per_b((rows, kd)), per_b((rows, 1)), per_b((rows, LANES)),
                  pl.BlockSpec((rows, nh), lambda b, j, pt: (0, 0)), per_b((kd, PAGE_SIZE)), per_b((kd, PAGE_SIZE)),
                  hbm, hbm, hbm],
        out_specs=per_b((rows, kd // FOX_KV_HEADS)),
        scratch_shapes=[pltpu.VMEM((2, pps, kd, PAGE_SIZE), F32), pltpu.VMEM((2, pps, kd, PAGE_SIZE), F32),
                        pltpu.VMEM((2, pps, nh, PAGE_SIZE), F32), pltpu.SemaphoreType.DMA((3, 2)),
                        pltpu.VMEM((rows, 1), F32), pltpu.VMEM((rows, 1), F32), pltpu.VMEM((rows, kd), F32),
                        pltpu.VMEM((rows, 1), F32)])
    return pl.pallas_call(
        functools.partial(_fox_paged_kernel, pps=pps, t_new=t_new, n_kv=FOX_KV_HEADS, layer=layer, n_pages=n_pages),
        grid_spec=gs,
        out_shape=jax.ShapeDtypeStruct((bsz, rows, kd // FOX_KV_HEADS), F32),
        compiler_params=_params("arbitrary", "arbitrary"),
        name="fox_paged",
    )(page_table, q_bd, cq, neg_cq, expand, k_new_t, v_new_t, k_pool_t, v_pool_t, lf_pool_t)


def _mem_attn_kernel(q_ref, k_ref, v_ref, o_ref, *, n_heads):
    hd = q_ref.shape[1] // n_heads
    for hh in range(n_heads):
        sl = slice(hh * hd, (hh + 1) * hd)
        s = _dot_nt(q_ref[:, sl], k_ref[:, sl])
        pe = jnp.exp(s - jnp.max(s, axis=-1, keepdims=True))
        p = (pe / jnp.sum(pe, axis=-1, keepdims=True)).astype(BF16)
        o_ref[:, sl] = _dot(p, v_ref[:, sl]).astype(o_ref.dtype)


def _mem_sample_kernel(qt_ref, k_ref, v_ref, o_ref, *, n_heads):
    cols = qt_ref.shape[2]
    tpad = cols // n_heads
    rows = k_ref.shape[1]
    r = lax.broadcasted_iota(jnp.int32, (rows, cols), 0)
    c = lax.broadcasted_iota(jnp.int32, (rows, cols), 1)
    keep = (r % n_heads) == (c // tpad)
    for b in range(qt_ref.shape[0]):
        st = jnp.where(keep, _dot(k_ref[b], qt_ref[b]), -jnp.inf)
        pe = jnp.exp(st - jnp.max(st, axis=0, keepdims=True))
        p = pe / jnp.sum(pe, axis=0, keepdims=True)
        o_ref[b] = _dot_tn(p, v_ref[b]).astype(o_ref.dtype)


def _mem_attn_sample(qt, mk, mv, group, layer):
    nb, d, cols = qt.shape
    rows = mk.shape[1]
    off = layer * (nb // group)
    return pl.pallas_call(
        functools.partial(_mem_sample_kernel, n_heads=MEM_HEADS),
        grid=(nb // group,),
        in_specs=[pl.BlockSpec((group, d, cols), lambda g: (g, 0, 0)),
                  pl.BlockSpec((group, rows, d), lambda g: (g + off, 0, 0)),
                  pl.BlockSpec((group, rows, d), lambda g: (g + off, 0, 0))],
        out_specs=pl.BlockSpec((group, cols, d), lambda g: (g, 0, 0)),
        out_shape=jax.ShapeDtypeStruct((nb, cols, d), F32),
        compiler_params=_params("parallel"),
        name="mem_attn_sample",
    )(qt, mk, mv)


def _mem_attn(q, mk, mv, n_seq, seq_len):
    w = q.shape[1]
    m = mk.shape[1]
    tq = _pick_tile(seq_len, TOKEN_TILE)
    nq = seq_len // tq
    return pl.pallas_call(
        functools.partial(_mem_attn_kernel, n_heads=MEM_HEADS),
        grid=(n_seq, nq),
        in_specs=[pl.BlockSpec((tq, w), lambda b, i: (b * nq + i, 0)),
                  pl.BlockSpec((None, m, w), lambda b, i: (b, 0, 0)),
                  pl.BlockSpec((None, m, w), lambda b, i: (b, 0, 0))],
        out_specs=pl.BlockSpec((tq, w), lambda b, i: (b * nq + i, 0)),
        out_shape=jax.ShapeDtypeStruct((n_seq * seq_len, w), BF16),
        compiler_params=_params("parallel", "parallel"),
        name="mem_attn",
    )(q, mk, mv)


def _rope_tables(pos):
    half = MLA_ROPE // 2
    inv_freq = ROPE_THETA ** (-jnp.arange(half, dtype=F32) / half)
    ang = pos.astype(F32)[:, None] * inv_freq[None, :]
    cos, sin = jnp.cos(ang), jnp.sin(ang)
    n = pos.shape[0]
    pad = jnp.zeros((n, LANES - MLA_NOPE - MLA_ROPE), F32)
    cos_t = jnp.concatenate([jnp.ones((n, MLA_NOPE), F32), cos, cos, pad], axis=-1)
    sin_t = jnp.concatenate([jnp.zeros((n, MLA_NOPE), F32), sin, sin, pad], axis=-1)
    return cos_t, sin_t


def _pad_axis(a, axis, size):
    pad = [(0, 0)] * a.ndim
    pad[axis] = (0, size - a.shape[axis])
    return jnp.pad(a, pad)


def kernel(x_prompt, x_sample, mem_prompt, cache_mla_latent, cache_mla_krope, state_hgrn, cache_fox_k, cache_fox_v, cache_fox_logf, cache_mem_k, cache_mem_v, page_table, norm_gains, w_ffn_gate, w_ffn_up, w_ffn_down, w_mem_q, w_mem_k, w_mem_v, w_mem_o, w_in_ab, g_mla_q, g_mla_kv, w_mla_uq, w_mla_uk, w_mla_uv, hgrn_lb_logits, g_hgrn_out, w_out_ab, w_in_c, b_fox_f, w_out_c):
    bp, tp, d = x_prompt.shape
    bs, ts, _ = x_sample.shape
    n_p, n_s = bp * tp, bs * ts
    depth = norm_gains.shape[0]
    n_pages = page_table.shape[1]
    m_tok = mem_prompt.shape[1]
    mem_w = MEM_HEADS * MEM_HD
    hg_w = HG_HEADS * HG_DK
    assert ts <= 8 and ts & (ts - 1) == 0

    x = jnp.concatenate([x_prompt.reshape(n_p, d), x_sample.reshape(n_s, d)], axis=0)
    pos = jnp.concatenate([jnp.tile(jnp.arange(tp), bp), n_pages * PAGE_SIZE + jnp.tile(jnp.arange(ts), bs)])
    cos_t, sin_t = _rope_tables(pos)
    cos_tt, sin_tt = cos_t.T, sin_t.T
    lb_all = jnp.cumsum(jax.nn.softmax(hgrn_lb_logits.astype(F32), axis=0), axis=0)
    kr_pool_t = jnp.swapaxes(cache_mla_krope, 2, 3)
    fk_pool_t = jnp.transpose(cache_fox_k, (0, 1, 3, 4, 2)).reshape(cache_fox_k.shape[0], -1, FOX_KV_HEADS * FOX_HD, PAGE_SIZE)
    fv_pool_t = jnp.transpose(cache_fox_v, (0, 1, 3, 4, 2)).reshape(cache_fox_v.shape[0], -1, FOX_KV_HEADS * FOX_HD, PAGE_SIZE)
    lf_pool_t = jnp.swapaxes(cache_fox_logf, 2, 3)
    mem_flat = mem_prompt.reshape(bp * m_tok, d)

    outs = {k: [] for k in ("lat_p", "kr_p", "hg_p", "fk_p", "fv_p", "flf_p", "mk_p", "mv_p",
                            "lat_s", "kr_s", "hg_s", "fk_s", "fv_s", "flf_s")}

    def sample_rows(a, width):
        return _pad_axis(a.reshape(bs, ts, width), 1, 8)

    for l in range(depth):
        ng = norm_gains[l]
        x = _ffn_half(x, ng[N_FFN1_PRE], ng[N_FFN1_POST], w_ffn_gate[l, 0], w_ffn_up[l, 0], w_ffn_down[l, 0])
        if l % 2 == 0:
            e = l // 2
            qt, kp, vt, c, krp, hq, logf, kk, hi, gate = _even_inproj(
                x, ng[N_MIX_PRE], cos_t, sin_t, cos_tt, sin_tt, w_in_ab[e], g_mla_q[e], g_mla_kv[e], w_mla_uq[e],
                w_mla_uk[e], w_mla_uv[e], lb_all[e], n_p)
            kr = krp[:, MLA_NOPE:MLA_NOPE + MLA_ROPE]
            oa_p = _flash_causal(qt, kp, vt, bp, tp, hps=MLA_FLASH_HEADS, hpk=1)
            ob_p, s_p = _hgrn(hq, kk, hi, logf, gate, jnp.zeros((bp, HG_HEADS, HG_DK, HG_DV), F32), g_hgrn_out[e],
                              bp, tp, _pick_tile(tp, HGRN_BLOCK), HGRN_CHUNK, BF16)
            w_o = w_out_ab[e]
            x = _outproj_residual(x, ng[N_MIX_POST], [oa_p, ob_p], [w_o[:MLA_HEADS * MLA_V], w_o[MLA_HEADS * MLA_V:]], 0)
            w_uk_t = jnp.zeros((MLA_HEADS, LANES, MLA_KV_LORA), F32).at[:, :MLA_NOPE].set(jnp.transpose(w_mla_uk[e], (1, 2, 0)))
            q_s = jnp.swapaxes(qt[:, n_p:].reshape(MLA_HEADS, LANES, n_s), 1, 2)
            qa = _headmm(q_s, w_uk_t, n_s, 0, BF16)
            qa = jnp.transpose(qa.reshape(MLA_HEADS, bs, ts, MLA_KV_LORA), (1, 0, 2, 3)).reshape(bs, MLA_HEADS * ts, MLA_KV_LORA)
            qr = q_s[:, :, MLA_NOPE:MLA_NOPE + MLA_ROPE].reshape(MLA_HEADS, bs, ts, MLA_ROPE)
            qr = jnp.transpose(qr, (1, 0, 2, 3)).reshape(bs, MLA_HEADS * ts, MLA_ROPE)
            c_new = _pad_axis(c[n_p:].reshape(bs, ts, MLA_KV_LORA), 1, PAGE_SIZE)
            kr_new_t = _pad_axis(jnp.swapaxes(kr[n_p:].reshape(bs, ts, MLA_ROPE), 1, 2), 2, PAGE_SIZE)
            att = _mla_paged(page_table, qa, qr, c_new, kr_new_t, cache_mla_latent, kr_pool_t, e, ts)
            att = jnp.transpose(att.reshape(bs, MLA_HEADS, ts, MLA_KV_LORA), (1, 0, 2, 3)).reshape(MLA_HEADS, n_s, MLA_KV_LORA)
            oa_s = _headmm(att, jnp.transpose(w_mla_uv[e], (1, 0, 2)), n_s, 0, BF16)
            oa_s = jnp.transpose(oa_s, (1, 0, 2)).reshape(n_s, MLA_HEADS * MLA_V)
            pads = [sample_rows(a[n_p:], hg_w).reshape(bs * 8, hg_w) for a in (hq, kk, hi, logf, gate)]
            ob_s, s_s = _hgrn(*pads, state_hgrn[e], g_hgrn_out[e], bs, 8, 8, 8, F32)
            ob_s = ob_s.reshape(bs, 8, hg_w)[:, :ts].reshape(n_s, hg_w)
            x = _outproj_residual(x, ng[N_MIX_POST], [oa_s, ob_s], [w_o[:MLA_HEADS * MLA_V], w_o[MLA_HEADS * MLA_V:]], n_p)
            outs["lat_p"].append(c[:n_p].reshape(bp, tp, MLA_KV_LORA))
            outs["kr_p"].append(kr[:n_p].reshape(bp, tp, MLA_ROPE))
            outs["hg_p"].append(s_p)
            outs["lat_s"].append(c[n_p:].reshape(bs, ts, MLA_KV_LORA))
            outs["kr_s"].append(kr[n_p:].reshape(bs, ts, MLA_ROPE))
            outs["hg_s"].append(s_s)
        else:
            o = l // 2
            qat, ka, v2t, kf, vf, logf, cum = _odd_inproj(x, ng[N_MIX_PRE], w_in_c[o], b_fox_f[o], n_p, tp, ts)
            att_p = _flash_causal(qat, ka, v2t, bp, tp, hps=FOX_FLASH_HEADS, hpk=FOX_HEADS // FOX_KV_HEADS)
            x = _outproj_residual(x, ng[N_MIX_POST], [att_p], [w_out_c[o]], 0)
            g = FOX_HEADS // FOX_KV_HEADS
            qs = qat[:, n_p:].reshape(FOX_HEADS, LANES, bs, ts)[:, :FOX_HD]
            onehot = jnp.asarray(np.arange(FOX_HEADS)[:, None] // g == np.arange(FOX_KV_HEADS)[None, :], BF16)
            q_bd = jnp.einsum('hdbt,hn->bhtnd', qs, onehot).reshape(bs, FOX_HEADS * ts, FOX_KV_HEADS * FOX_HD)
            cum_s = cum[n_p:, :FOX_HEADS].reshape(bs, ts, FOX_HEADS)
            cq = jnp.swapaxes(cum_s, 1, 2).reshape(bs, FOX_HEADS * ts, 1)
            neg_cq = jnp.broadcast_to(-jnp.swapaxes(cum_s, 1, 2)[:, :, None, :], (bs, FOX_HEADS, ts, ts))
            neg_cq = _pad_axis(neg_cq.reshape(bs, FOX_HEADS * ts, ts), 2, LANES)
            expand = jnp.asarray(np.repeat(np.eye(FOX_HEADS), ts, axis=0), BF16)
            k_new_t = _pad_axis(jnp.swapaxes(kf[n_p:].reshape(bs, ts, -1), 1, 2), 2, PAGE_SIZE)
            v_new_t = _pad_axis(jnp.swapaxes(vf[n_p:].reshape(bs, ts, -1), 1, 2), 2, PAGE_SIZE)
            att_s = _fox_paged(page_table, q_bd, cq, neg_cq, expand, k_new_t, v_new_t, fk_pool_t, fv_pool_t, lf_pool_t, o, ts)
            att_s = jnp.transpose(att_s.reshape(bs, FOX_HEADS, ts, FOX_HD), (0, 2, 1, 3)).reshape(n_s, FOX_HEADS * FOX_HD)
            x = _outproj_residual(x, ng[N_MIX_POST], [att_s], [w_out_c[o]], n_p)
            lf = logf[:, :FOX_HEADS]
            outs["fk_p"].append(kf[:n_p].reshape(bp, tp, FOX_KV_HEADS, FOX_HD))
            outs["fv_p"].append(vf[:n_p].reshape(bp, tp, FOX_KV_HEADS, FOX_HD))
            outs["flf_p"].append(lf[:n_p].reshape(bp, tp, FOX_HEADS))
            outs["fk_s"].append(kf[n_p:].reshape(bs, ts, FOX_KV_HEADS, FOX_HD))
            outs["fv_s"].append(vf[n_p:].reshape(bs, ts, FOX_KV_HEADS, FOX_HD))
            outs["flf_s"].append(lf[n_p:].reshape(bs, ts, FOX_HEADS))
        mk, mv = _norm_matmul(mem_flat, ng[N_MEM_SRC], [w_mem_k[l], w_mem_v[l]], [1.0, 1.0], [F32, F32])
        mk, mv = mk.reshape(bp, m_tok, mem_w), mv.reshape(bp, m_tok, mem_w)
        (mq,) = _norm_matmul(x, ng[N_MEM_PRE], [w_mem_q[l]], [float(MEM_HD ** -0.5)], [F32])
        om_p = _mem_attn(mq, mk, mv, bp, tp)
        x = _outproj_residual(x, ng[N_MEM_POST], [om_p], [w_mem_o[l]], 0)
        grp = _pick_tile(bs, MEM_GROUP) if bs % 8 == 0 else 1
        mq_t = jnp.transpose(sample_rows(mq[n_p:], mem_w).reshape(bs, 8, MEM_HEADS, MEM_HD), (0, 3, 2, 1))
        om_s = _mem_attn_sample(mq_t.reshape(bs, MEM_HD, MEM_HEADS * 8).astype(BF16),
                                cache_mem_k.reshape(depth * bs, m_tok * MEM_HEADS, MEM_HD),
                                cache_mem_v.reshape(depth * bs, m_tok * MEM_HEADS, MEM_HD), grp, l)
        om_s = jnp.transpose(om_s.reshape(bs, MEM_HEADS, 8, MEM_HD)[:, :, :ts], (0, 2, 1, 3)).reshape(n_s, mem_w)
        x = _outproj_residual(x, ng[N_MEM_POST], [om_s], [w_mem_o[l]], n_p)
        outs["mk_p"].append(mk.reshape(bp, m_tok, MEM_HEADS, MEM_HD))
        outs["mv_p"].append(mv.reshape(bp, m_tok, MEM_HEADS, MEM_HD))
        x = _ffn_half(x, ng[N_FFN2_PRE], ng[N_FFN2_POST], w_ffn_gate[l, 1], w_ffn_up[l, 1], w_ffn_down[l, 1])

    st = lambda k: jnp.stack(outs[k])
    return (x[:n_p].reshape(bp, tp, d), x[n_p:].reshape(bs, ts, d),
            st("lat_p"), st("kr_p"), st("hg_p"), st("fk_p"), st("fv_p"), st("flf_p"), st("mk_p"), st("mv_p"),
            st("lat_s"), st("kr_s"), st("hg_s"), st("fk_s"), st("fv_s"), st("flf_s"))
```
